```python
import math
import jax, jax.numpy as jnp
from jax import lax
import numpy as np


D_MODEL = 1024
BATCH = 8
SEQ = 4096
DEPTH = 2

CHUNK = 64
D_MIX = D_MODEL
D_A = 3 * D_MIX // 8
A_BLOCK_DIM = 64
A_BLOCKS = D_A // A_BLOCK_DIM
CONV_WIDTH = 4
RG_C = 8.0
B_HEAD_DIM = 64
D_B = 3 * D_MIX // 8
B_HEADS = D_B // B_HEAD_DIM
Q_BLOCK = 128
D_C = D_MIX - D_A - D_B
C_GROUP = 16
C_GROUPS = D_C // C_GROUP
C_STATE = 64
N_IN = 2 * D_A + 3 * D_B + B_HEADS + D_C
D_FF = 2816
ALPHA = (2 * DEPTH) ** 0.25
BETA = (8 * DEPTH) ** -0.25
LN_EPS = 1e-5
RMS_EPS = 1e-6

kernel_name = "hybrid_rglru_fox_s5_macaron_deepnorm"


def layer_norm(x, g, b):
    xf = x.astype(jnp.float32)
    mu = jnp.mean(xf, axis=-1, keepdims=True)
    var = jnp.mean(jnp.square(xf - mu), axis=-1, keepdims=True)
    return ((xf - mu) * lax.rsqrt(var + LN_EPS) * g + b).astype(x.dtype)


def rms_norm(x, g):
    xf = x.astype(jnp.float32)
    return xf * lax.rsqrt(jnp.mean(xf * xf, axis=-1, keepdims=True) + RMS_EPS) * g


def swiglu(x, w_gate, w_up, w_down):
    return (jax.nn.silu(x @ w_gate) * (x @ w_up)) @ w_down


def causal_depthwise_conv(x, w, b):
    s = x.shape[1]
    xp = jnp.pad(x, ((0, 0), (CONV_WIDTH - 1, 0), (0, 0)))
    out = b
    for k in range(CONV_WIDTH):
        out = out + w[k] * xp[:, k:k + s]
    return out


def linear_scan(a, b):
    def combine(left, right):
        a_l, b_l = left
        a_r, b_r = right
        return a_l * a_r, a_r * b_l + b_r
    return lax.associative_scan(combine, (a, b), axis=1)[1]


def rg_lru(x, w_a, b_a, w_x, b_x, lam):
    bsz, s, _ = x.shape
    xb = x.reshape(bsz, s, A_BLOCKS, A_BLOCK_DIM)
    r = jax.nn.sigmoid(jnp.einsum('bshi,hij->bshj', xb, w_a).reshape(bsz, s, D_A) + b_a)
    i = jax.nn.sigmoid(jnp.einsum('bshi,hij->bshj', xb, w_x).reshape(bsz, s, D_A) + b_x)
    log_a = -RG_C * r * jax.nn.softplus(-lam)
    a = jnp.exp(log_a)
    gated = jnp.sqrt(-jnp.expm1(2.0 * log_a)) * (i * x)
    return linear_scan(a, gated)


def forgetting_attention(q, k, v, log_f):
    s = q.shape[2]
    scale = B_HEAD_DIM ** -0.5
    c = jnp.cumsum(log_f, axis=-1)
    outs = []
    for blk in range(s // Q_BLOCK):
        q0 = blk * Q_BLOCK
        q1 = q0 + Q_BLOCK
        qb = q[:, :, q0:q1]
        kb = k[:, :, :q1]
        vb = v[:, :, :q1]
        logits = jnp.einsum('bhqd,bhkd->bhqk', qb, kb) * scale
        logits = logits + (c[:, :, q0:q1, None] - c[:, :, None, :q1])
        mask = (q0 + jnp.arange(Q_BLOCK))[:, None] >= jnp.arange(q1)[None, :]
        logits = jnp.where(mask, logits, -jnp.inf)
        p = jax.nn.softmax(logits, axis=-1)
        outs.append(jnp.einsum('bhqk,bhkd->bhqd', p, vb))
    return jnp.concatenate(outs, axis=2)


def s5_ssm(u, a_re, a_im, log_dt, b_re, b_im, c_re, c_im, d, w_glu):
    f32 = jnp.float32
    bsz, s, _ = u.shape
    lam = lax.complex(a_re.astype(f32), a_im.astype(f32))
    dt = jnp.exp(log_dt.astype(f32))[:, None]
    a_bar = jnp.exp(lam * dt)
    b_bar = ((a_bar - 1.0) / lam)[:, :, None] * lax.complex(b_re.astype(f32), b_im.astype(f32))
    c_mat = lax.complex(c_re.astype(f32), c_im.astype(f32))
    ug = u.astype(f32).reshape(bsz, s, C_GROUPS, C_GROUP)
    bu = jnp.einsum('bsgc,gpc->bsgp', ug.astype(jnp.complex64), b_bar)
    a_seq = jnp.broadcast_to(a_bar, (1, s, C_GROUPS, C_STATE))
    h = linear_scan(a_seq, bu)
    y = jnp.einsum('bsgp,gcp->bsgc', h, c_mat).real + d.astype(f32).reshape(C_GROUPS, C_GROUP) * ug
    y = jax.nn.gelu(y.reshape(bsz, s, D_C))
    return y * jax.nn.sigmoid(y @ w_glu.astype(f32))


def hybrid_mixer(x, w_in, conv_w, conv_b, rg_w_a, rg_b_a, rg_w_x, rg_b_x, rg_lambda, fox_b_f,
                 s5_a_re, s5_a_im, s5_log_dt, s5_b_re, s5_b_im, s5_c_re, s5_c_im, s5_d, s5_w_glu,
                 mix_norm_g, w_out):
    f32 = jnp.float32
    bsz, s, _ = x.shape
    z = x @ w_in
    cuts = [int(v) for v in np.cumsum([D_A, D_A, D_B, D_B, D_B, B_HEADS])]
    a_x, a_gate, q, k, v, f_logit, c_u = jnp.split(z, cuts, axis=-1)

    xa = causal_depthwise_conv(a_x, conv_w, conv_b).astype(f32)
    out_a = jax.nn.gelu(a_gate.astype(f32)) * rg_lru(xa, rg_w_a, rg_b_a, rg_w_x, rg_b_x, rg_lambda)

    def heads(t):
        return t.astype(f32).reshape(bsz, s, B_HEADS, B_HEAD_DIM).transpose(0, 2, 1, 3)
    log_f = jax.nn.log_sigmoid(f_logit.astype(f32) + fox_b_f).transpose(0, 2, 1)
    out_b = forgetting_attention(heads(q), heads(k), heads(v), log_f)
    out_b = out_b.transpose(0, 2, 1, 3).reshape(bsz, s, D_B)

    out_c = s5_ssm(c_u, s5_a_re, s5_a_im, s5_log_dt, s5_b_re, s5_b_im, s5_c_re, s5_c_im, s5_d, s5_w_glu)

    g_a, g_b, g_c = jnp.split(mix_norm_g, [D_A, D_A + D_B])
    o = jnp.concatenate([rms_norm(out_a, g_a), rms_norm(out_b, g_b), rms_norm(out_c, g_c)], axis=-1)
    return o.astype(x.dtype) @ w_out


def setup_inputs(seed: int = 0) -> dict:
    key = jax.random.key(seed)
    ks = jax.random.split(key, 40)
    L = DEPTH
    f32 = jnp.float32

    def nrm(k, shape, scale):
        return scale * jax.random.normal(k, shape, f32)

    u_a = jax.random.uniform(ks[11], (L, D_A), f32, minval=0.9, maxval=0.999)
    a0 = u_a ** (1.0 / RG_C)
    rg_lambda = jnp.log(a0) - jnp.log1p(-a0)
    s5_a_im = jnp.pi * jnp.arange(C_STATE, dtype=f32)[None, None, :] + nrm(ks[14], (L, C_GROUPS, C_STATE), 0.01)
    s5_log_dt = jax.random.uniform(ks[15], (L, C_GROUPS), f32, minval=math.log(1e-3), maxval=math.log(1e-1))
    return {
        'x': jax.random.normal(ks[0], (BATCH, SEQ, D_MODEL), f32),
        'ffn1_w_gate': nrm(ks[1], (L, D_MODEL, D_FF), D_MODEL ** -0.5),
        'ffn1_w_up': nrm(ks[2], (L, D_MODEL, D_FF), D_MODEL ** -0.5),
        'ffn1_w_down': nrm(ks[3], (L, D_FF, D_MODEL), BETA * D_FF ** -0.5),
        'ln1_g': 1.0 + nrm(ks[4], (L, D_MODEL), 0.02),
        'ln1_b': nrm(ks[5], (L, D_MODEL), 0.02),
        'w_in': nrm(ks[6], (L, D_MODEL, N_IN), D_MODEL ** -0.5),
        'conv_w': nrm(ks[7], (L, CONV_WIDTH, D_A), CONV_WIDTH ** -0.5),
        'conv_b': nrm(ks[8], (L, D_A), 0.02),
        'rg_w_a': nrm(ks[9], (L, A_BLOCKS, A_BLOCK_DIM, A_BLOCK_DIM), A_BLOCK_DIM ** -0.5),
        'rg_b_a': nrm(ks[10], (L, D_A), 0.02),
        'rg_w_x': nrm(ks[12], (L, A_BLOCKS, A_BLOCK_DIM, A_BLOCK_DIM), A_BLOCK_DIM ** -0.5),
        'rg_b_x': nrm(ks[16], (L, D_A), 0.02),
        'rg_lambda': rg_lambda,
        'fox_b_f': 2.0 + nrm(ks[17], (L, B_HEADS), 0.1),
        's5_a_re': -0.5 + nrm(ks[13], (L, C_GROUPS, C_STATE), 0.01),
        's5_a_im': s5_a_im,
        's5_log_dt': s5_log_dt,
        's5_b_re': nrm(ks[18], (L, C_GROUPS, C_STATE, C_GROUP), (2 * C_GROUP) ** -0.5),
        's5_b_im': nrm(ks[19], (L, C_GROUPS, C_STATE, C_GROUP), (2 * C_GROUP) ** -0.5),
        's5_c_re': nrm(ks[20], (L, C_GROUPS, C_GROUP, C_STATE), (2 * C_STATE) ** -0.5),
        's5_c_im': nrm(ks[21], (L, C_GROUPS, C_GROUP, C_STATE), (2 * C_STATE) ** -0.5),
        's5_d': nrm(ks[22], (L, D_C), 1.0),
        's5_w_glu': nrm(ks[23], (L, D_C, D_C), D_C ** -0.5),
        'mix_norm_g': 1.0 + nrm(ks[24], (L, D_MIX), 0.02),
        'w_out': nrm(ks[25], (L, D_MIX, D_MODEL), BETA * D_MIX ** -0.5),
        'ln2_g': 1.0 + nrm(ks[26], (L, D_MODEL), 0.02),
        'ln2_b': nrm(ks[27], (L, D_MODEL), 0.02),
        'ffn2_w_gate': nrm(ks[28], (L, D_MODEL, D_FF), D_MODEL ** -0.5),
        'ffn2_w_up': nrm(ks[29], (L, D_MODEL, D_FF), D_MODEL ** -0.5),
        'ffn2_w_down': nrm(ks[30], (L, D_FF, D_MODEL), BETA * D_FF ** -0.5),
        'ln3_g': 1.0 + nrm(ks[31], (L, D_MODEL), 0.02),
        'ln3_b': nrm(ks[32], (L, D_MODEL), 0.02),
    }


def reference(x, ffn1_w_gate, ffn1_w_up, ffn1_w_down, ln1_g, ln1_b,
              w_in, conv_w, conv_b, rg_w_a, rg_b_a, rg_w_x, rg_b_x, rg_lambda, fox_b_f,
              s5_a_re, s5_a_im, s5_log_dt, s5_b_re, s5_b_im, s5_c_re, s5_c_im, s5_d, s5_w_glu,
              mix_norm_g, w_out, ln2_g, ln2_b,
              ffn2_w_gate, ffn2_w_up, ffn2_w_down, ln3_g, ln3_b):
    for l in range(DEPTH):
        x = layer_norm(ALPHA * x + 0.5 * swiglu(x, ffn1_w_gate[l], ffn1_w_up[l], ffn1_w_down[l]),
                       ln1_g[l], ln1_b[l])
        mix = hybrid_mixer(x, w_in[l], conv_w[l], conv_b[l], rg_w_a[l], rg_b_a[l], rg_w_x[l], rg_b_x[l],
                           rg_lambda[l], fox_b_f[l],
                           s5_a_re[l], s5_a_im[l], s5_log_dt[l], s5_b_re[l], s5_b_im[l],
                           s5_c_re[l], s5_c_im[l], s5_d[l], s5_w_glu[l],
                           mix_norm_g[l], w_out[l])
        x = layer_norm(ALPHA * x + mix, ln2_g[l], ln2_b[l])
        x = layer_norm(ALPHA * x + 0.5 * swiglu(x, ffn2_w_gate[l], ffn2_w_up[l], ffn2_w_down[l]),
                       ln3_g[l], ln3_b[l])
    return x
```

```python
import functools
import math

import numpy as np
import jax
import jax.numpy as jnp
from jax import lax
from jax.experimental import pallas as pl
from jax.experimental.pallas import tpu as pltpu

F32 = jnp.float32
BF16 = jnp.bfloat16

D_MODEL = 1024
D_FF = 2816
D_A = 384
A_BLOCKS = 6
A_BLOCK_DIM = 64
CONV_WIDTH = 4
RG_C = 8.0
D_B = 384
B_HEADS = 6
B_HEAD_DIM = 64
HEAD_PAIRS = B_HEADS // 2
D_C = 256
C_GROUP = 16
C_GROUPS = 16
C_STATE = 64
N_MAIN = 2 * D_A + 3 * D_B + D_C
F_ROWS = 8 * HEAD_PAIRS
DEPTH = 2
ALPHA = (2 * DEPTH) ** 0.25
LN_EPS = 1e-5
RMS_EPS = 1e-6

S5_CHUNK = 8
S5_HALF = C_GROUPS // 2
S5_HCOLS = S5_CHUNK * S5_HALF * C_GROUP
S5_HSTATE = S5_HALF * C_STATE

VMEM_LIMIT = 56 * 1024 * 1024


def _cparams(*sem):
    return pltpu.CompilerParams(dimension_semantics=sem, vmem_limit_bytes=VMEM_LIMIT)


def _const_spec(shape):
    nd = len(shape)
    return pl.BlockSpec(shape, lambda *_: (0,) * nd, pipeline_mode=pl.Buffered(1))


def _layer_norm(r, g, b):
    mu = jnp.mean(r, axis=-1, keepdims=True)
    d = r - mu
    var = jnp.mean(d * d, axis=-1, keepdims=True)
    return d * lax.rsqrt(var + LN_EPS) * g + b


def _rms_norm(x, g):
    return x * lax.rsqrt(jnp.mean(x * x, axis=-1, keepdims=True) + RMS_EPS) * g


def _gelu(x):
    return 0.5 * x * (1.0 + jnp.tanh(math.sqrt(2.0 / math.pi) * (x + 0.044715 * (x * x * x))))


def _log_sigmoid(x):
    return -(jnp.maximum(-x, 0.0) + jnp.log1p(jnp.exp(-jnp.abs(x))))


def _ffn_ln_kernel(x_ref, wg_ref, wu_ref, wd_ref, g_ref, b_ref, o_ref):
    x = x_ref[...]
    xb = x.astype(BF16)
    gate = jnp.dot(xb, wg_ref[...], preferred_element_type=F32)
    up = jnp.dot(xb, wu_ref[...], preferred_element_type=F32)
    h = (gate * jax.nn.sigmoid(gate) * up).astype(BF16)
    y = jnp.dot(h, wd_ref[...], preferred_element_type=F32)
    o_ref[...] = _layer_norm(ALPHA * x + 0.5 * y, g_ref[...], b_ref[...])


def _ffn_ln(x2d, wg, wu, wd, g, b, tm):
    n = x2d.shape[0]
    row = pl.BlockSpec((tm, D_MODEL), lambda i: (i, 0))
    return pl.pallas_call(
        _ffn_ln_kernel,
        grid=(n // tm,),
        in_specs=[row, _const_spec((D_MODEL, D_FF)), _const_spec((D_MODEL, D_FF)),
                  _const_spec((D_FF, D_MODEL)), _const_spec((1, D_MODEL)), _const_spec((1, D_MODEL))],
        out_specs=row,
        out_shape=jax.ShapeDtypeStruct((n, D_MODEL), F32),
        compiler_params=_cparams("parallel"),
        name="ffn_ln",
    )(x2d, wg, wu, wd, g, b)


def _in_proj_kernel(x_ref, w_ref, wft_ref, bf_ref, ax_ref, ag_ref, q_ref, k_ref, v_ref, cu_ref, lf_ref):
    xb = x_ref[...].astype(BF16)
    z = jnp.dot(xb, w_ref[...], preferred_element_type=F32)
    ax_ref[...] = z[:, 0:D_A]
    ag_ref[...] = z[:, D_A:2 * D_A]
    o = 2 * D_A
    q_ref[...] = (z[:, o:o + D_B] * (B_HEAD_DIM ** -0.5)).astype(BF16)
    k_ref[...] = z[:, o + D_B:o + 2 * D_B].astype(BF16)
    v_ref[...] = z[:, o + 2 * D_B:o + 3 * D_B].astype(BF16)
    cu_ref[...] = z[:, o + 3 * D_B:o + 3 * D_B + D_C]
    f = lax.dot_general(wft_ref[...], xb, (((1,), (1,)), ((), ())), preferred_element_type=F32)
    lf_ref[0] = _log_sigmoid(f + bf_ref[...])


def _in_proj(x2d, w_main, wft, bft, bsz, seq, tm):
    ns = seq // tm
    n = bsz * seq
    tok = lambda width: pl.BlockSpec((tm, width), lambda b, s: (b * ns + s, 0))
    tmajor = pl.BlockSpec((tm, D_A), lambda b, s: (s, b))
    return pl.pallas_call(
        _in_proj_kernel,
        grid=(bsz, ns),
        in_specs=[tok(D_MODEL), _const_spec((D_MODEL, N_MAIN)), _const_spec((F_ROWS, D_MODEL)),
                  _const_spec((F_ROWS, 1))],
        out_specs=[tmajor, tmajor, tok(D_B), tok(D_B), tok(D_B), tok(D_C),
                   pl.BlockSpec((1, F_ROWS, tm), lambda b, s: (b, 0, s))],
        out_shape=[jax.ShapeDtypeStruct((seq, bsz * D_A), F32),
                   jax.ShapeDtypeStruct((seq, bsz * D_A), F32),
                   jax.ShapeDtypeStruct((n, D_B), BF16),
                   jax.ShapeDtypeStruct((n, D_B), BF16),
                   jax.ShapeDtypeStruct((n, D_B), BF16),
                   jax.ShapeDtypeStruct((n, D_C), F32),
                   jax.ShapeDtypeStruct((bsz, F_ROWS, seq), F32)],
        compiler_params=_cparams("parallel", "parallel"),
        name="in_proj",
    )(x2d, w_main, wft, bft)


def _cumsum_kernel(lf_ref, c_ref):
    y = lf_ref[0]
    seq = y.shape[-1]
    lane = lax.broadcasted_iota(jnp.int32, y.shape, 1)
    d = 1
    while d < seq:
        y = y + jnp.where(lane >= d, pltpu.roll(y, d, axis=1), 0.0)
        d *= 2
    c_ref[0] = y


def _cumsum(lf):
    bsz, rows, seq = lf.shape
    spec = pl.BlockSpec((1, rows, seq), lambda b: (b, 0, 0))
    return pl.pallas_call(
        _cumsum_kernel, grid=(bsz,), in_specs=[spec], out_specs=spec,
        out_shape=jax.ShapeDtypeStruct(lf.shape, F32),
        compiler_params=_cparams("parallel"), name="fox_cumsum",
    )(lf)


def _rglru_kernel(ax_ref, ag_ref, cw_ref, cb_ref, wg_ref, bg_ref, sp_ref, gn_ref, o_ref,
                  tail_sc, h_sc, a_sc, b_sc):
    ts, bsz, _ = ax_ref.shape

    @pl.when(pl.program_id(0) == 0)
    def _():
        tail_sc[...] = jnp.zeros_like(tail_sc)
        h_sc[...] = jnp.zeros_like(h_sc)

    ax = ax_ref[...]
    xext = jnp.concatenate([tail_sc[...], ax], axis=0)
    xa = cb_ref[...]
    for k in range(CONV_WIDTH):
        xa = xa + cw_ref[k] * xext[k:k + ts]
    tail_sc[...] = ax[ts - (CONV_WIDTH - 1):]

    xa2 = xa.reshape(ts * bsz, D_A)
    gates = jnp.dot(xa2.astype(BF16), wg_ref[...], preferred_element_type=F32) + bg_ref[...]
    r = jax.nn.sigmoid(gates[:, :D_A])
    i = jax.nn.sigmoid(gates[:, D_A:])
    log_a = (-RG_C) * r * sp_ref[...]
    a = jnp.exp(log_a)
    one_minus_a2 = -jnp.tanh(log_a) * (1.0 + a * a)
    a_sc[...] = a.reshape(ts, bsz, D_A)
    b_sc[...] = (jnp.sqrt(one_minus_a2) * (i * xa2)).reshape(ts, bsz, D_A)

    def step(t, h):
        h = a_sc[t] * h + b_sc[t]
        b_sc[t] = h
        return h

    h_sc[...] = lax.fori_loop(0, ts, step, h_sc[...], unroll=8)

    out = _gelu(ag_ref[...].reshape(ts * bsz, D_A)) * b_sc[...].reshape(ts * bsz, D_A)
    o_ref[...] = _rms_norm(out, gn_ref[...]).reshape(ts, bsz, D_A)


def _rglru(ax3, ag3, cw, cb, wg, bg, sp, gn, ts):
    seq, bsz, _ = ax3.shape
    blk = pl.BlockSpec((ts, bsz, D_A), lambda s: (s, 0, 0))
    return pl.pallas_call(
        _rglru_kernel,
        grid=(seq // ts,),
        in_specs=[blk, blk, _const_spec((CONV_WIDTH, 1, D_A)), _const_spec((1, D_A)),
                  _const_spec((D_A, 2 * D_A)), _const_spec((1, 2 * D_A)), _const_spec((1, D_A)),
                  _const_spec((1, D_A))],
        out_specs=blk,
        out_shape=jax.ShapeDtypeStruct(ax3.shape, F32),
        scratch_shapes=[pltpu.VMEM((CONV_WIDTH - 1, bsz, D_A), F32), pltpu.VMEM((bsz, D_A), F32),
                        pltpu.VMEM((ts, bsz, D_A), F32), pltpu.VMEM((ts, bsz, D_A), F32)],
        compiler_params=_cparams("arbitrary"),
        name="rglru",
    )(ax3, ag3, cw, cb, wg, bg, sp, gn)


def _fox_kernel(q_ref, k_ref, v_ref, c_ref, o_ref, m_sc, l_sc, acc_sc, *, tq):
    qi = pl.program_id(2)
    q = q_ref[0]
    lane = lax.broadcasted_iota(jnp.int32, (tq, 2 * B_HEAD_DIM), 1)
    q0 = pl.multiple_of(qi * tq, tq)
    c_here = c_ref[0, :, pl.ds(q0, tq)]
    m_sc[...] = jnp.full_like(m_sc, -jnp.inf)
    l_sc[...] = jnp.zeros_like(l_sc)
    acc_sc[...] = jnp.zeros_like(acc_sc)

    def block(k0, diag):
        k = k_ref[0, pl.ds(k0, tq), :]
        v = v_ref[0, pl.ds(k0, tq), :]
        c_k = c_ref[0, :, pl.ds(k0, tq)]
        for j in range(2):
            kj = jnp.where((lane >= B_HEAD_DIM) == (j == 1), k, jnp.zeros_like(k))
            s = lax.dot_general(q, kj, (((1,), (1,)), ((), ())), preferred_element_type=F32)
            s = s - (c_k[j:j + 1, :] - c_here[j:j + 1, 0:1])
            if diag:
                row = lax.broadcasted_iota(jnp.int32, (tq, tq), 0)
                col = lax.broadcasted_iota(jnp.int32, (tq, tq), 1)
                s = jnp.where(col <= row, s, -jnp.inf)
            m_prev = m_sc[j]
            m_new = jnp.maximum(m_prev, jnp.max(s, axis=-1, keepdims=True))
            alpha = jnp.exp(m_prev - m_new)
            p = jnp.exp(s - m_new)
            l_sc[j] = alpha * l_sc[j] + jnp.sum(p, axis=-1, keepdims=True)
            acc_sc[j] = alpha * acc_sc[j] + jnp.dot(p.astype(BF16), v, preferred_element_type=F32)
            m_sc[j] = m_new

    def body(ki, carry):
        block(pl.multiple_of(ki * tq, tq), False)
        return carry

    lax.fori_loop(0, qi, body, 0)
    block(q0, True)
    o0 = acc_sc[0] / l_sc[0]
    o1 = acc_sc[1] / l_sc[1]
    o_ref[0] = jnp.where(lane < B_HEAD_DIM, o0, o1)


def _fox(q3, k3, v3, c, tq):
    bsz, seq, _ = q3.shape
    qspec = pl.BlockSpec((1, tq, 2 * B_HEAD_DIM), lambda b, h, i: (b, i, h))
    kvspec = pl.BlockSpec((1, seq, 2 * B_HEAD_DIM), lambda b, h, i: (b, 0, h))
    return pl.pallas_call(
        functools.partial(_fox_kernel, tq=tq),
        grid=(bsz, HEAD_PAIRS, seq // tq),
        in_specs=[qspec, kvspec, kvspec, pl.BlockSpec((1, 8, seq), lambda b, h, i: (b, h, 0))],
        out_specs=qspec,
        out_shape=jax.ShapeDtypeStruct((bsz, seq, D_B), F32),
        scratch_shapes=[pltpu.VMEM((2, tq, 1), F32), pltpu.VMEM((2, tq, 1), F32),
                        pltpu.VMEM((2, tq, 2 * B_HEAD_DIM), F32)],
        compiler_params=_cparams("parallel", "parallel", "arbitrary"),
        name="fox_attention",
    )(q3, k3, v3, c)


def _shift_rows(x, d, row):
    return jnp.where(row >= d, pltpu.roll(x, d, axis=0), 0.0)


def _s5_kernel(u_ref, m_ref, bm_ref, cm_ref, ap_ref, d_ref, wglu_ref, gn_ref, o_ref, *, levels):
    u = u_ref[...]
    rows = u.shape[0]
    ub = u.astype(BF16)
    row = lax.broadcasted_iota(jnp.int32, (rows, S5_HSTATE), 0)
    y_halves = []
    for h in range(2):
        uh = jnp.concatenate([ub[:, t * D_C + h * 128: t * D_C + (h + 1) * 128] for t in range(S5_CHUNK)],
                             axis=1)
        y = jnp.dot(uh, m_ref[h], preferred_element_type=F32)
        st = jnp.dot(uh, bm_ref[h], preferred_element_type=F32)
        hre = st[:, :S5_HSTATE]
        him = st[:, S5_HSTATE:]
        for lv in range(levels):
            d = 1 << lv
            ar = ap_ref[lv:lv + 1, (2 * h) * S5_HSTATE:(2 * h + 1) * S5_HSTATE]
            ai = ap_ref[lv:lv + 1, (2 * h + 1) * S5_HSTATE:(2 * h + 2) * S5_HSTATE]
            sre = _shift_rows(hre, d, row)
            sim = _shift_rows(him, d, row)
            hre, him = hre + (ar * sre - ai * sim), him + (ar * sim + ai * sre)
        hprev = jnp.concatenate([_shift_rows(hre, 1, row), _shift_rows(him, 1, row)], axis=1)
        y = y + jnp.dot(hprev.astype(BF16), cm_ref[h], preferred_element_type=F32)
        y_halves.append(y)
    outs = []
    for t in range(S5_CHUNK):
        yt = jnp.concatenate([y_halves[0][:, t * 128:(t + 1) * 128], y_halves[1][:, t * 128:(t + 1) * 128]], axis=1)
        yt = _gelu(yt + d_ref[...] * u[:, t * D_C:(t + 1) * D_C])
        gl = jnp.dot(yt.astype(BF16), wglu_ref[...], preferred_element_type=F32)
        outs.append(_rms_norm(yt * jax.nn.sigmoid(gl), gn_ref[...]))
    o_ref[...] = jnp.concatenate(outs, axis=1)


def _s5(u2, m, bm, cm, ap, d, wglu, gn, rows_per_seq):
    n, width = u2.shape
    levels = ap.shape[0]
    blk = pl.BlockSpec((rows_per_seq, width), lambda b: (b, 0))
    return pl.pallas_call(
        functools.partial(_s5_kernel, levels=levels),
        grid=(n // rows_per_seq,),
        in_specs=[blk, _const_spec(m.shape), _const_spec(bm.shape), _const_spec(cm.shape),
                  _const_spec(ap.shape), _const_spec((1, D_C)), _const_spec((D_C, D_C)), _const_spec((1, D_C))],
        out_specs=blk,
        out_shape=jax.ShapeDtypeStruct(u2.shape, F32),
        compiler_params=_cparams("parallel"),
        name="s5",
    )(u2, m, bm, cm, ap, d, wglu, gn)


def _s5_prepare(a_re, a_im, log_dt, b_re, b_im, c_re, c_im, n_chunks):
    T = S5_CHUNK
    lam = lax.complex(a_re, a_im)
    dt = jnp.exp(log_dt)[:, None]
    ldt = lam * dt
    a_bar = jnp.exp(ldt)
    b_bar = ((a_bar - 1.0) / lam)[:, :, None] * lax.complex(b_re, b_im)
    c_mat = lax.complex(c_re, c_im)
    apow = jnp.exp(ldt[None] * jnp.arange(T + 1, dtype=F32)[:, None, None])
    cb = c_mat[None, :, None, :, :] * jnp.transpose(b_bar, (0, 2, 1))[None, :, :, None, :]
    kern = jnp.real(jnp.sum(cb * apow[:T, :, None, None, :], axis=-1))
    tin = jnp.arange(T)[:, None]
    tout = jnp.arange(T)[None, :]
    lag = tout - tin
    ktt = jnp.where((lag >= 0)[:, :, None, None, None], kern[jnp.clip(lag, 0, T - 1)], 0.0)
    eye = jnp.eye(S5_HALF, dtype=F32)
    sin = apow[T - 1 - jnp.arange(T)][:, :, :, None] * b_bar[None]
    sout = c_mat[None] * apow[1:T + 1][:, :, None, :]
    ms, bms, cms = [], [], []
    for h in range(2):
        gs = slice(h * S5_HALF, (h + 1) * S5_HALF)
        kh = ktt[:, :, gs]
        mh = jnp.einsum('abgio,gk->agibko', kh, eye)
        ms.append(mh.reshape(S5_HCOLS, S5_HCOLS))
        sh = sin[:, gs]
        parts = [jnp.einsum('agpi,gk->agikp', f(sh), eye).reshape(S5_HCOLS, S5_HSTATE)
                 for f in (jnp.real, jnp.imag)]
        bms.append(jnp.concatenate(parts, axis=1))
        so = sout[:, gs]
        parts = [jnp.einsum('bgop,gk->gpbko', f(so), eye).reshape(S5_HSTATE, S5_HCOLS)
                 for f in (jnp.real, lambda z: -jnp.imag(z))]
        cms.append(jnp.concatenate(parts, axis=0))
    levels = max(1, int(math.ceil(math.log2(n_chunks))))
    steps = (T * (2 ** np.arange(levels))).astype(np.float32)
    alev = jnp.exp(ldt[None] * jnp.asarray(steps)[:, None, None])
    aps = []
    for h in range(2):
        gs = slice(h * S5_HALF, (h + 1) * S5_HALF)
        aps += [jnp.real(alev[:, gs]).reshape(levels, S5_HSTATE), jnp.imag(alev[:, gs]).reshape(levels, S5_HSTATE)]
    return (jnp.stack(ms).astype(BF16), jnp.stack(bms).astype(BF16), jnp.stack(cms).astype(BF16),
            jnp.concatenate(aps, axis=1))


def _out_proj_kernel(oa_ref, ob_ref, oc_ref, x_ref, gb_ref, w_ref, g_ref, b_ref, o_ref):
    nb = _rms_norm(ob_ref[...], gb_ref[...])
    o = jnp.concatenate([oa_ref[...], nb, oc_ref[...]], axis=1).astype(BF16)
    mix = jnp.dot(o, w_ref[...], preferred_element_type=F32)
    o_ref[...] = _layer_norm(ALPHA * x_ref[...] + mix, g_ref[...], b_ref[...])


def _out_proj(oa2, ob2, oc2, x2d, gb, w, g, b, bsz, seq, tm):
    ns = seq // tm
    tok = lambda width: pl.BlockSpec((tm, width), lambda bb, s: (bb * ns + s, 0))
    return pl.pallas_call(
        _out_proj_kernel,
        grid=(bsz, ns),
        in_specs=[pl.BlockSpec((tm, D_A), lambda bb, s: (s, bb)), tok(D_B), tok(D_C), tok(D_MODEL),
                  _const_spec((1, D_B)), _const_spec((D_MODEL, D_MODEL)), _const_spec((1, D_MODEL)),
                  _const_spec((1, D_MODEL))],
        out_specs=tok(D_MODEL),
        out_shape=jax.ShapeDtypeStruct(x2d.shape, F32),
        compiler_params=_cparams("parallel", "parallel"),
        name="out_proj_ln",
    )(oa2, ob2, oc2, x2d, gb, w, g, b)


def _block_diag(w):
    eye = jnp.eye(A_BLOCKS, dtype=w.dtype)
    return jnp.einsum('hij,hk->hikj', w, eye).reshape(D_A, D_A)


def _pick(n, pref):
    t = min(n, pref)
    while n % t:
        t //= 2
    return t


def kernel(x, ffn1_w_gate, ffn1_w_up, ffn1_w_down, ln1_g, ln1_b, w_in, conv_w, conv_b, rg_w_a, rg_b_a, rg_w_x, rg_b_x, rg_lambda, fox_b_f, s5_a_re, s5_a_im, s5_log_dt, s5_b_re, s5_b_im, s5_c_re, s5_c_im, s5_d, s5_w_glu, mix_norm_g, w_out, ln2_g, ln2_b, ffn2_w_gate, ffn2_w_up, ffn2_w_down, ln3_g, ln3_b):
    bsz, seq, _ = x.shape
    n = bsz * seq
    tm = _pick(seq, 512)
    ts = _pick(seq, 256)
    tq = _pick(seq, 512)
    n_chunks = seq // S5_CHUNK
    row = lambda v: v.reshape(1, -1)
    h = x.reshape(n, D_MODEL)
    cuts = np.cumsum([D_A, D_A, D_B, D_B, D_B, B_HEADS])
    for l in range(DEPTH):
        h = _ffn_ln(h, ffn1_w_gate[l].astype(BF16), ffn1_w_up[l].astype(BF16), ffn1_w_down[l].astype(BF16),
                    row(ln1_g[l]), row(ln1_b[l]), tm)

        wl = w_in[l]
        w_main = jnp.concatenate([wl[:, :cuts[4]], wl[:, cuts[5]:]], axis=1).astype(BF16)
        wf = wl[:, cuts[4]:cuts[5]].T
        wft = jnp.zeros((HEAD_PAIRS, 8, D_MODEL), F32).at[:, :2].set(wf.reshape(HEAD_PAIRS, 2, D_MODEL))
        bft = jnp.zeros((HEAD_PAIRS, 8), F32).at[:, :2].set(fox_b_f[l].reshape(HEAD_PAIRS, 2))
        ax, ag, q, k, v, cu, lf = _in_proj(h, w_main, wft.reshape(F_ROWS, D_MODEL).astype(BF16),
                                           bft.reshape(F_ROWS, 1), bsz, seq, tm)

        wgates = jnp.concatenate([_block_diag(rg_w_a[l]), _block_diag(rg_w_x[l])], axis=1).astype(BF16)
        bgates = jnp.concatenate([rg_b_a[l], rg_b_x[l]]).reshape(1, 2 * D_A)
        g_a, g_b, g_c = jnp.split(mix_norm_g[l], [D_A, D_A + D_B])
        oa = _rglru(ax.reshape(seq, bsz, D_A), ag.reshape(seq, bsz, D_A),
                    conv_w[l].reshape(CONV_WIDTH, 1, D_A), row(conv_b[l]), wgates, bgates,
                    row(jax.nn.softplus(-rg_lambda[l])), row(g_a), ts)

        c = _cumsum(lf)
        ob = _fox(q.reshape(bsz, seq, D_B), k.reshape(bsz, seq, D_B), v.reshape(bsz, seq, D_B), c, tq)

        m, bm, cm, ap = _s5_prepare(s5_a_re[l], s5_a_im[l], s5_log_dt[l], s5_b_re[l], s5_b_im[l],
                                    s5_c_re[l], s5_c_im[l], n_chunks)
        oc = _s5(cu.reshape(n // S5_CHUNK, S5_CHUNK * D_C), m, bm, cm, ap, row(s5_d[l]),
                 s5_w_glu[l].astype(BF16), row(g_c), n_chunks)

        h = _out_proj(oa.reshape(seq, bsz * D_A), ob.reshape(n, D_B), oc.reshape(n, D_C), h, row(g_b),
                      w_out[l].astype(BF16), row(ln2_g[l]), row(ln2_b[l]), bsz, seq, tm)
        h = _ffn_ln(h, ffn2_w_gate[l].astype(BF16), ffn2_w_up[l].astype(BF16), ffn2_w_down[l].astype(BF16),
                    row(ln3_g[l]), row(ln3_b[l]), tm)
    return h.reshape(bsz, seq, D_MODEL)
```

```python
import functools
import math

import numpy as np
import jax
import jax.numpy as jnp
from jax import lax
from jax.experimental import pallas as pl
from jax.experimental.pallas import tpu as pltpu

F32 = jnp.float32
BF16 = jnp.bfloat16

D_MODEL = 1024
D_FF = 2816
D_A = 384
A_BLOCKS = 6
A_BLOCK_DIM = 64
CONV_WIDTH = 4
RG_C = 8.0
D_B = 384
B_HEADS = 6
B_HEAD_DIM = 64
HEAD_PAIRS = B_HEADS // 2
D_C = 256
C_GROUP = 16
C_GROUPS = 16
C_STATE = 64
N_MAIN = 2 * D_A + 3 * D_B + D_C + 128
F_LANE0 = 64
LOG2E = math.log2(math.e)
QSCALE = LOG2E * B_HEAD_DIM ** -0.5
NEG = -1e30
ROWS = 128
DEPTH = 2
ALPHA = (2 * DEPTH) ** 0.25
LN_EPS = 1e-5
RMS_EPS = 1e-6

S5_CHUNK = 8
S5_HALF = C_GROUPS // 2
S5_HCOLS = S5_CHUNK * S5_HALF * C_GROUP
S5_HSTATE = S5_HALF * C_STATE

VMEM_LIMIT = 56 * 1024 * 1024


def _cparams(*sem):
    return pltpu.CompilerParams(dimension_semantics=sem, vmem_limit_bytes=VMEM_LIMIT)


def _const_spec(shape):
    nd = len(shape)
    return pl.BlockSpec(shape, lambda *_: (0,) * nd, pipeline_mode=pl.Buffered(1))


def _layer_norm(r, g, b):
    mu = jnp.mean(r, axis=-1, keepdims=True)
    d = r - mu
    var = jnp.mean(d * d, axis=-1, keepdims=True)
    return d * lax.rsqrt(var + LN_EPS) * g + b


def _rms_norm(x, g):
    return x * lax.rsqrt(jnp.mean(x * x, axis=-1, keepdims=True) + RMS_EPS) * g


def _gelu(x):
    return 0.5 * x * (1.0 + jnp.tanh(math.sqrt(2.0 / math.pi) * (x + 0.044715 * (x * x * x))))


def _log_sigmoid(x):
    return -(jnp.maximum(-x, 0.0) + jnp.log1p(jnp.exp(-jnp.abs(x))))


def _shift_rows(x, d, row):
    return jnp.where(row >= d, pltpu.roll(x, d, axis=0), 0.0)


def _ffn_ln_kernel(x_ref, wg_ref, wu_ref, wd_ref, g_ref, b_ref, o_ref):
    x = x_ref[...]
    xb = x.astype(BF16)
    gate = jnp.dot(xb, wg_ref[...], preferred_element_type=F32)
    up = jnp.dot(xb, wu_ref[...], preferred_element_type=F32)
    h = (gate * jax.nn.sigmoid(gate) * up).astype(BF16)
    y = jnp.dot(h, wd_ref[...], preferred_element_type=F32)
    o_ref[...] = _layer_norm(ALPHA * x + 0.5 * y, g_ref[...], b_ref[...])


def _ffn_ln(x2d, wg, wu, wd, g, b, tm):
    n = x2d.shape[0]
    row = pl.BlockSpec((tm, D_MODEL), lambda i: (i, 0))
    return pl.pallas_call(
        _ffn_ln_kernel,
        grid=(n // tm,),
        in_specs=[row, _const_spec((D_MODEL, D_FF)), _const_spec((D_MODEL, D_FF)),
                  _const_spec((D_FF, D_MODEL)), _const_spec((1, D_MODEL)), _const_spec((1, D_MODEL))],
        out_specs=row,
        out_shape=jax.ShapeDtypeStruct((n, D_MODEL), F32),
        compiler_params=_cparams("parallel"),
        name="ffn_ln",
    )(x2d, wg, wu, wd, g, b)


def _in_proj_kernel(x_ref, w_ref, bf_ref, fmask_ref, ones_ref, ax_ref, ag_ref, qa_ref, ka_ref, v_ref, cu_ref,
                    base_ref, carry_sc):
    tm = x_ref.shape[0]

    @pl.when(pl.program_id(1) == 0)
    def _():
        carry_sc[...] = jnp.zeros_like(carry_sc)

    xb = x_ref[...].astype(BF16)
    z = jnp.dot(xb, w_ref[...], preferred_element_type=F32)
    ax_ref[...] = z[:, 0:D_A]
    ag_ref[...] = z[:, D_A:2 * D_A]
    o = 2 * D_A
    v_ref[...] = z[:, o + 2 * D_B:o + 3 * D_B].astype(BF16)
    cu_ref[...] = z[:, o + 3 * D_B:o + 3 * D_B + D_C]

    lf = _log_sigmoid(z[:, N_MAIN - 128:] + bf_ref[...]) * fmask_ref[...] * LOG2E
    row = lax.broadcasted_iota(jnp.int32, (tm, 128), 0)
    lane = lax.broadcasted_iota(jnp.int32, (tm, 128), 1)
    c = lf
    d = 1
    while d < tm:
        c = c + _shift_rows(c, d, row)
        d *= 2
    carry = carry_sc[...]
    rows = [jnp.broadcast_to(carry[:, F_LANE0 + h:F_LANE0 + h + 1], (1, 128)) for h in range(B_HEADS)]
    pad = jnp.zeros((6, 128), F32)
    base_ref[0, 0] = jnp.stack([jnp.concatenate([rows[2 * p], rows[2 * p + 1], pad], axis=0)
                                for p in range(HEAD_PAIRS)])
    carry_sc[...] = carry + c[tm - 1:tm, :]

    hi = c.astype(BF16).astype(F32)
    r1 = c - hi
    mid = r1.astype(BF16).astype(F32)
    lo = (r1 - mid).astype(BF16).astype(F32)
    pieces = jnp.where(lane < F_LANE0 + 8, hi, jnp.where(lane < F_LANE0 + 16, mid, lo))
    for p in range(HEAD_PAIRS):
        qt = z[:, o + 128 * p:o + 128 * (p + 1)]
        kt = z[:, o + D_B + 128 * p:o + D_B + 128 * (p + 1)]
        for e in range(2):
            h = 2 * p + e
            qh = pltpu.roll(qt, B_HEAD_DIM, axis=1) if e else qt
            kh = pltpu.roll(kt, B_HEAD_DIM, axis=1) if e else kt
            qa_ref[0, h] = jnp.where(lane < B_HEAD_DIM, qh * QSCALE, ones_ref[h:h + 1, :]).astype(BF16)
            ka_ref[0, h] = jnp.where(lane < B_HEAD_DIM, kh, -pieces).astype(BF16)


def _in_proj(x2d, w_main, bf_row, fmask, ones, bsz, seq, tm):
    ns = seq // tm
    n = bsz * seq
    tok = lambda width: pl.BlockSpec((tm, width), lambda b, s: (b * ns + s, 0))
    tmajor = pl.BlockSpec((tm, D_A), lambda b, s: (s, b))
    heads = pl.BlockSpec((1, B_HEADS, tm, 128), lambda b, s: (b, 0, s, 0))
    return pl.pallas_call(
        _in_proj_kernel,
        grid=(bsz, ns),
        in_specs=[tok(D_MODEL), _const_spec((D_MODEL, N_MAIN)), _const_spec((1, 128)), _const_spec((1, 128)),
                  _const_spec((B_HEADS, 128))],
        out_specs=[tmajor, tmajor, heads, heads, tok(D_B), tok(D_C),
                   pl.BlockSpec((1, 1, HEAD_PAIRS, 8, 128), lambda b, s: (b, s, 0, 0, 0))],
        out_shape=[jax.ShapeDtypeStruct((seq, bsz * D_A), F32),
                   jax.ShapeDtypeStruct((seq, bsz * D_A), F32),
                   jax.ShapeDtypeStruct((bsz, B_HEADS, seq, 128), BF16),
                   jax.ShapeDtypeStruct((bsz, B_HEADS, seq, 128), BF16),
                   jax.ShapeDtypeStruct((n, D_B), BF16),
                   jax.ShapeDtypeStruct((n, D_C), F32),
                   jax.ShapeDtypeStruct((bsz, ns, HEAD_PAIRS, 8, 128), F32)],
        scratch_shapes=[pltpu.VMEM((1, 128), F32)],
        compiler_params=_cparams("parallel", "arbitrary"),
        name="in_proj",
    )(x2d, w_main, bf_row, fmask, ones)


def _rglru_kernel(ax_ref, ag_ref, cw_ref, cb_ref, wg_ref, bg_ref, sp_ref, gn_ref, o_ref,
                  tail_sc, h_sc, a_sc, b_sc):
    ts, bsz, _ = ax_ref.shape

    @pl.when(pl.program_id(0) == 0)
    def _():
        tail_sc[...] = jnp.zeros_like(tail_sc)
        h_sc[...] = jnp.zeros_like(h_sc)

    ax = ax_ref[...]
    xext = jnp.concatenate([tail_sc[...], ax], axis=0)
    xa = cb_ref[...]
    for k in range(CONV_WIDTH):
        xa = xa + cw_ref[k] * xext[k:k + ts]
    tail_sc[...] = ax[ts - (CONV_WIDTH - 1):]

    xa2 = xa.reshape(ts * bsz, D_A)
    gates = jnp.dot(xa2.astype(BF16), wg_ref[...], preferred_element_type=F32) + bg_ref[...]
    r = jax.nn.sigmoid(gates[:, :D_A])
    i = jax.nn.sigmoid(gates[:, D_A:])
    log_a = (-RG_C) * r * sp_ref[...]
    a = jnp.exp(log_a)
    one_minus_a2 = -jnp.tanh(log_a) * (1.0 + a * a)
    a_sc[...] = a.reshape(ts, bsz, D_A)
    b_sc[...] = (jnp.sqrt(one_minus_a2) * (i * xa2)).reshape(ts, bsz, D_A)

    def step(t, h):
        h = a_sc[t] * h + b_sc[t]
        b_sc[t] = h
        return h

    h_sc[...] = lax.fori_loop(0, ts, step, h_sc[...], unroll=8)

    out = _gelu(ag_ref[...].reshape(ts * bsz, D_A)) * b_sc[...].reshape(ts * bsz, D_A)
    o_ref[...] = _rms_norm(out, gn_ref[...]).reshape(ts, bsz, D_A)


def _rglru(ax3, ag3, cw, cb, wg, bg, sp, gn, ts):
    seq, bsz, _ = ax3.shape
    blk = pl.BlockSpec((ts, bsz, D_A), lambda s: (s, 0, 0))
    return pl.pallas_call(
        _rglru_kernel,
        grid=(seq // ts,),
        in_specs=[blk, blk, _const_spec((CONV_WIDTH, 1, D_A)), _const_spec((1, D_A)),
                  _const_spec((D_A, 2 * D_A)), _const_spec((1, 2 * D_A)), _const_spec((1, D_A)),
                  _const_spec((1, D_A))],
        out_specs=blk,
        out_shape=jax.ShapeDtypeStruct(ax3.shape, F32),
        scratch_shapes=[pltpu.VMEM((CONV_WIDTH - 1, bsz, D_A), F32), pltpu.VMEM((bsz, D_A), F32),
                        pltpu.VMEM((ts, bsz, D_A), F32), pltpu.VMEM((ts, bsz, D_A), F32)],
        compiler_params=_cparams("arbitrary"),
        name="rglru",
    )(ax3, ag3, cw, cb, wg, bg, sp, gn)


def _fox_kernel(qa_ref, ka_ref, v_ref, base_ref, o_ref, vaug_sc, acc_sc, m_sc, *, tq):
    qi = pl.program_id(2)
    seq = v_ref.shape[1]
    lane = lax.broadcasted_iota(jnp.int32, (ROWS, 128), 1)
    sub = lax.broadcasted_iota(jnp.int32, (ROWS, 128), 0)

    @pl.when(qi == 0)
    def _():
        lane_v = lax.broadcasted_iota(jnp.int32, (tq, 128), 1)
        for c in range(seq // tq):
            vc = v_ref[0, c * tq:(c + 1) * tq, :].astype(F32)
            vaug_sc[0, c * tq:(c + 1) * tq, :] = jnp.concatenate(
                [jnp.where(lane_v < B_HEAD_DIM, vc, 0.0), jnp.where(lane_v == 0, 1.0, 0.0)], axis=1).astype(BF16)
            vaug_sc[1, c * tq:(c + 1) * tq, :] = jnp.concatenate(
                [jnp.where(lane_v >= B_HEAD_DIM, vc, 0.0), jnp.where(lane_v == 1, 1.0, 0.0)], axis=1).astype(BF16)

    acc_sc[...] = jnp.zeros_like(acc_sc)
    m_sc[...] = jnp.full_like(m_sc, NEG)
    base_q = base_ref[0, qi, 0]

    def block(k0, base_k, diag):
        deltas = [base_q[j:j + 1, :] - base_k[j:j + 1, :] for j in range(2)]
        for r in range(tq // ROWS):
            rows = slice(r * ROWS, (r + 1) * ROWS)
            n = (r + 1) * ROWS if diag else tq
            ps, alphas = [], []
            for j in range(2):
                s = lax.dot_general(qa_ref[0, j, rows, :], ka_ref[0, j, pl.ds(k0, n), :],
                                    (((1,), (1,)), ((), ())), preferred_element_type=F32)
                tiles = [s[:, c * 128:(c + 1) * 128] for c in range(n // 128)]
                if diag:
                    tiles[-1] = jnp.where(lane <= sub, tiles[-1], NEG)
                tmax = tiles[0]
                for t in tiles[1:]:
                    tmax = jnp.maximum(tmax, t)
                m_cur = jnp.max(tmax, axis=-1, keepdims=True) + deltas[j]
                m_prev = m_sc[j, rows, :]
                m_new = jnp.maximum(m_prev, m_cur)
                alphas.append(jnp.exp2(m_prev - m_new))
                m_sc[j, rows, :] = m_new
                mt = m_new - deltas[j]
                ps.append(jnp.concatenate([jnp.exp2(t - mt) for t in tiles], axis=1).astype(BF16))
            vk = jnp.concatenate([vaug_sc[0, pl.ds(k0, n), :], vaug_sc[1, pl.ds(k0, n), :]], axis=0)
            pv = jnp.dot(jnp.concatenate(ps, axis=1), vk, preferred_element_type=F32)
            a = jnp.concatenate([jnp.where(lane < B_HEAD_DIM, alphas[0], alphas[1]),
                                 jnp.where(lane == 0, alphas[0], alphas[1])], axis=1)
            acc_sc[rows, :] = a * acc_sc[rows, :] + pv

    def body(ki, carry):
        block(pl.multiple_of(ki * tq, tq), base_ref[0, ki, 0], False)
        return carry

    lax.fori_loop(0, qi, body, 0)
    block(pl.multiple_of(qi * tq, tq), base_q, True)

    acc = acc_sc[...]
    lane_o = lax.broadcasted_iota(jnp.int32, (tq, 128), 1)
    o_ref[0] = acc[:, :128] / jnp.where(lane_o < B_HEAD_DIM, acc[:, 128:129], acc[:, 129:130])


def _fox(qa, ka, v3, base, tq):
    bsz, seq, _ = v3.shape
    ns = seq // tq
    return pl.pallas_call(
        functools.partial(_fox_kernel, tq=tq),
        grid=(bsz, HEAD_PAIRS, ns),
        in_specs=[pl.BlockSpec((1, 2, tq, 128), lambda b, h, i: (b, h, i, 0)),
                  pl.BlockSpec((1, 2, seq, 128), lambda b, h, i: (b, h, 0, 0)),
                  pl.BlockSpec((1, seq, 128), lambda b, h, i: (b, 0, h)),
                  pl.BlockSpec((1, ns, 1, 8, 128), lambda b, h, i: (b, 0, h, 0, 0))],
        out_specs=pl.BlockSpec((1, tq, 128), lambda b, h, i: (b, i, h)),
        out_shape=jax.ShapeDtypeStruct((bsz, seq, D_B), F32),
        scratch_shapes=[pltpu.VMEM((2, seq, 256), BF16), pltpu.VMEM((tq, 256), F32),
                        pltpu.VMEM((2, tq, 128), F32)],
        compiler_params=_cparams("parallel", "parallel", "arbitrary"),
        name="fox_attention",
    )(qa, ka, v3, base)


def _s5_kernel(u_ref, m_ref, bm_ref, cm_ref, ap_ref, d_ref, wglu_ref, gn_ref, o_ref, *, levels):
    u = u_ref[...]
    rows = u.shape[0]
    ub = u.astype(BF16)
    row = lax.broadcasted_iota(jnp.int32, (rows, S5_HSTATE), 0)
    y_halves = []
    for h in range(2):
        uh = jnp.concatenate([ub[:, t * D_C + h * 128: t * D_C + (h + 1) * 128] for t in range(S5_CHUNK)],
                             axis=1)
        y = jnp.dot(uh, m_ref[h], preferred_element_type=F32)
        st = jnp.dot(uh, bm_ref[h], preferred_element_type=F32)
        hre = st[:, :S5_HSTATE]
        him = st[:, S5_HSTATE:]
        for lv in range(levels):
            d = 1 << lv
            ar = ap_ref[lv:lv + 1, (2 * h) * S5_HSTATE:(2 * h + 1) * S5_HSTATE]
            ai = ap_ref[lv:lv + 1, (2 * h + 1) * S5_HSTATE:(2 * h + 2) * S5_HSTATE]
            sre = _shift_rows(hre, d, row)
            sim = _shift_rows(him, d, row)
            hre, him = hre + (ar * sre - ai * sim), him + (ar * sim + ai * sre)
        hprev = jnp.concatenate([_shift_rows(hre, 1, row), _shift_rows(him, 1, row)], axis=1)
        y = y + jnp.dot(hprev.astype(BF16), cm_ref[h], preferred_element_type=F32)
        y_halves.append(y)
    outs = []
    for t in range(S5_CHUNK):
        yt = jnp.concatenate([y_halves[0][:, t * 128:(t + 1) * 128], y_halves[1][:, t * 128:(t + 1) * 128]], axis=1)
        yt = _gelu(yt + d_ref[...] * u[:, t * D_C:(t + 1) * D_C])
        gl = jnp.dot(yt.astype(BF16), wglu_ref[...], preferred_element_type=F32)
        outs.append(_rms_norm(yt * jax.nn.sigmoid(gl), gn_ref[...]))
    o_ref[...] = jnp.concatenate(outs, axis=1)


def _s5(u2, m, bm, cm, ap, d, wglu, gn, rows_per_seq):
    n, width = u2.shape
    levels = ap.shape[0]
    blk = pl.BlockSpec((rows_per_seq, width), lambda b: (b, 0))
    return pl.pallas_call(
        functools.partial(_s5_kernel, levels=levels),
        grid=(n // rows_per_seq,),
        in_specs=[blk, _const_spec(m.shape), _const_spec(bm.shape), _const_spec(cm.shape),
                  _const_spec(ap.shape), _const_spec((1, D_C)), _const_spec((D_C, D_C)), _const_spec((1, D_C))],
        out_specs=blk,
        out_shape=jax.ShapeDtypeStruct(u2.shape, F32),
        compiler_params=_cparams("parallel"),
        name="s5",
    )(u2, m, bm, cm, ap, d, wglu, gn)


def _s5_prepare(a_re, a_im, log_dt, b_re, b_im, c_re, c_im, n_chunks):
    hi = lax.Precision.HIGHEST
    T = S5_CHUNK
    dt = jnp.exp(log_dt)[:, None]
    lr, li = a_re * dt, a_im * dt

    def cpow(tau):
        mag = jnp.exp(lr[None] * tau[:, None, None])
        ang = li[None] * tau[:, None, None]
        return mag * jnp.cos(ang), mag * jnp.sin(ang)

    ar, ai = cpow(jnp.arange(T + 1, dtype=F32))
    nr, ni = ar[1] - 1.0, ai[1]
    den = a_re * a_re + a_im * a_im
    fr, fi = (nr * a_re + ni * a_im) / den, (ni * a_re - nr * a_im) / den
    bbr = fr[:, :, None] * b_re - fi[:, :, None] * b_im
    bbi = fr[:, :, None] * b_im + fi[:, :, None] * b_re
    abr = ar[:, :, :, None] * bbr[None] - ai[:, :, :, None] * bbi[None]
    abi = ar[:, :, :, None] * bbi[None] + ai[:, :, :, None] * bbr[None]
    kern = (jnp.einsum('gop,tgpi->tgio', c_re, abr[:T], precision=hi)
            - jnp.einsum('gop,tgpi->tgio', c_im, abi[:T], precision=hi))
    tin = jnp.arange(T)[:, None]
    tout = jnp.arange(T)[None, :]
    lag = tout - tin
    ktt = jnp.where((lag >= 0)[:, :, None, None, None], kern[jnp.clip(lag, 0, T - 1)], 0.0)
    eye = jnp.eye(S5_HALF, dtype=F32)
    rev = T - 1 - jnp.arange(T)
    sin_r, sin_i = abr[rev], abi[rev]
    so_r = c_re[None] * ar[1:, :, None, :] - c_im[None] * ai[1:, :, None, :]
    so_i = c_re[None] * ai[1:, :, None, :] + c_im[None] * ar[1:, :, None, :]
    ms, bms, cms = [], [], []
    for h in range(2):
        gs = slice(h * S5_HALF, (h + 1) * S5_HALF)
        mh = jnp.einsum('abgio,gk->agibko', ktt[:, :, gs], eye)
        ms.append(mh.reshape(S5_HCOLS, S5_HCOLS))
        parts = [jnp.einsum('agpi,gk->agikp', s[:, gs], eye).reshape(S5_HCOLS, S5_HSTATE) for s in (sin_r, sin_i)]
        bms.append(jnp.concatenate(parts, axis=1))
        parts = [jnp.einsum('bgop,gk->gpbko', s[:, gs], eye).reshape(S5_HSTATE, S5_HCOLS) for s in (so_r, -so_i)]
        cms.append(jnp.concatenate(parts, axis=0))
    levels = max(1, int(math.ceil(math.log2(n_chunks))))
    lev_r, lev_i = cpow(jnp.asarray((T * (2 ** np.arange(levels))).astype(np.float32)))
    aps = []
    for h in range(2):
        gs = slice(h * S5_HALF, (h + 1) * S5_HALF)
        aps += [lev_r[:, gs].reshape(levels, S5_HSTATE), lev_i[:, gs].reshape(levels, S5_HSTATE)]
    return (jnp.stack(ms).astype(BF16), jnp.stack(bms).astype(BF16), jnp.stack(cms).astype(BF16),
            jnp.concatenate(aps, axis=1))


def _out_proj_kernel(oa_ref, ob_ref, oc_ref, x_ref, gb_ref, w_ref, g_ref, b_ref, o_ref):
    nb = _rms_norm(ob_ref[...], gb_ref[...])
    o = jnp.concatenate([oa_ref[...], nb, oc_ref[...]], axis=1).astype(BF16)
    mix = jnp.dot(o, w_ref[...], preferred_element_type=F32)
    o_ref[...] = _layer_norm(ALPHA * x_ref[...] + mix, g_ref[...], b_ref[...])


def _out_proj(oa2, ob2, oc2, x2d, gb, w, g, b, bsz, seq, tm):
    ns = seq // tm
    tok = lambda width: pl.BlockSpec((tm, width), lambda bb, s: (bb * ns + s, 0))
    return pl.pallas_call(
        _out_proj_kernel,
        grid=(bsz, ns),
        in_specs=[pl.BlockSpec((tm, D_A), lambda bb, s: (s, bb)), tok(D_B), tok(D_C), tok(D_MODEL),
                  _const_spec((1, D_B)), _const_spec((D_MODEL, D_MODEL)), _const_spec((1, D_MODEL)),
                  _const_spec((1, D_MODEL))],
        out_specs=tok(D_MODEL),
        out_shape=jax.ShapeDtypeStruct(x2d.shape, F32),
        compiler_params=_cparams("parallel", "parallel"),
        name="out_proj_ln",
    )(oa2, ob2, oc2, x2d, gb, w, g, b)


def _block_diag(w):
    eye = jnp.eye(A_BLOCKS, dtype=w.dtype)
    return jnp.einsum('hij,hk->hikj', w, eye).reshape(D_A, D_A)


def _pick(n, pref):
    t = min(n, pref)
    while n % t:
        t //= 2
    return t


def kernel(x, ffn1_w_gate, ffn1_w_up, ffn1_w_down, ln1_g, ln1_b, w_in, conv_w, conv_b, rg_w_a, rg_b_a, rg_w_x, rg_b_x, rg_lambda, fox_b_f, s5_a_re, s5_a_im, s5_log_dt, s5_b_re, s5_b_im, s5_c_re, s5_c_im, s5_d, s5_w_glu, mix_norm_g, w_out, ln2_g, ln2_b, ffn2_w_gate, ffn2_w_up, ffn2_w_down, ln3_g, ln3_b):
    bsz, seq, _ = x.shape
    n = bsz * seq
    tm = _pick(seq, 512)
    ts = _pick(seq, 256)
    n_chunks = seq // S5_CHUNK
    row = lambda v: v.reshape(1, -1)
    h = x.reshape(n, D_MODEL)
    cuts = np.cumsum([D_A, D_A, D_B, D_B, D_B, B_HEADS])
    for l in range(DEPTH):
        h = _ffn_ln(h, ffn1_w_gate[l].astype(BF16), ffn1_w_up[l].astype(BF16), ffn1_w_down[l].astype(BF16),
                    row(ln1_g[l]), row(ln1_b[l]), tm)

        wl = w_in[l]
        fcols = jnp.zeros((D_MODEL, 128), F32)
        bf_row = jnp.zeros((128,), F32)
        fmask = np.zeros((1, 128), np.float32)
        ones = np.zeros((B_HEADS, 128), np.float32)
        for piece in range(3):
            lo_ = F_LANE0 + 8 * piece
            fcols = fcols.at[:, lo_:lo_ + B_HEADS].set(wl[:, cuts[4]:cuts[5]])
            bf_row = bf_row.at[lo_:lo_ + B_HEADS].set(fox_b_f[l])
            fmask[0, lo_:lo_ + B_HEADS] = 1.0
            ones[np.arange(B_HEADS), lo_ + np.arange(B_HEADS)] = 1.0
        w_main = jnp.concatenate([wl[:, :cuts[4]], wl[:, cuts[5]:], fcols], axis=1).astype(BF16)
        ax, ag, qa, ka, v, cu, base = _in_proj(h, w_main, bf_row.reshape(1, 128), jnp.asarray(fmask),
                                               jnp.asarray(ones), bsz, seq, tm)

        wgates = jnp.concatenate([_block_diag(rg_w_a[l]), _block_diag(rg_w_x[l])], axis=1).astype(BF16)
        bgates = jnp.concatenate([rg_b_a[l], rg_b_x[l]]).reshape(1, 2 * D_A)
        g_a, g_b, g_c = jnp.split(mix_norm_g[l], [D_A, D_A + D_B])
        oa = _rglru(ax.reshape(seq, bsz, D_A), ag.reshape(seq, bsz, D_A),
                    conv_w[l].reshape(CONV_WIDTH, 1, D_A), row(conv_b[l]), wgates, bgates,
                    row(jax.nn.softplus(-rg_lambda[l])), row(g_a), ts)

        ob = _fox(qa, ka, v.reshape(bsz, seq, D_B), base, tm)

        m, bm, cm, ap = _s5_prepare(s5_a_re[l], s5_a_im[l], s5_log_dt[l], s5_b_re[l], s5_b_im[l],
                                    s5_c_re[l], s5_c_im[l], n_chunks)
        oc = _s5(cu.reshape(n // S5_CHUNK, S5_CHUNK * D_C), m, bm, cm, ap, row(s5_d[l]),
                 s5_w_glu[l].astype(BF16), row(g_c), n_chunks)

        h = _out_proj(oa.reshape(seq, bsz * D_A), ob.reshape(n, D_B), oc.reshape(n, D_C), h, row(g_b),
                      w_out[l].astype(BF16), row(ln2_g[l]), row(ln2_b[l]), bsz, seq, tm)
        h = _ffn_ln(h, ffn2_w_gate[l].astype(BF16), ffn2_w_up[l].astype(BF16), ffn2_w_down[l].astype(BF16),
                    row(ln3_g[l]), row(ln3_b[l]), tm)
    return h.reshape(bsz, seq, D_MODEL)
```

```python
import functools
import math

import numpy as np
import jax
import jax.numpy as jnp
from jax import lax
from jax.experimental import pallas as pl
from jax.experimental.pallas import tpu as pltpu

F32 = jnp.float32
BF16 = jnp.bfloat16

D_MODEL = 1024
D_FF = 2816
D_A = 384
A_BLOCKS = 6
A_BLOCK_DIM = 64
CONV_WIDTH = 4
RG_C = 8.0
D_B = 384
B_HEADS = 6
B_HEAD_DIM = 64
HEAD_PAIRS = B_HEADS // 2
D_C = 256
C_GROUP = 16
C_GROUPS = 16
C_STATE = 64
N_MAIN = 2 * D_A + 3 * D_B + D_C + 128
F_LANE0 = 64
LOG2E = math.log2(math.e)
QSCALE = LOG2E * B_HEAD_DIM ** -0.5
NEG = -1e30
ROWS = 128
DEPTH = 2
ALPHA = (2 * DEPTH) ** 0.25
LN_EPS = 1e-5
RMS_EPS = 1e-6

S5_CHUNK = 8
S5_HALF = C_GROUPS // 2
S5_HCOLS = S5_CHUNK * S5_HALF * C_GROUP
S5_HSTATE = S5_HALF * C_STATE

VMEM_LIMIT = 56 * 1024 * 1024


def _cparams(*sem):
    return pltpu.CompilerParams(dimension_semantics=sem, vmem_limit_bytes=VMEM_LIMIT)


def _const_spec(shape):
    nd = len(shape)
    return pl.BlockSpec(shape, lambda *_: (0,) * nd, pipeline_mode=pl.Buffered(1))


def _layer_norm(r, g, b):
    mu = jnp.mean(r, axis=-1, keepdims=True)
    d = r - mu
    var = jnp.mean(d * d, axis=-1, keepdims=True)
    return d * lax.rsqrt(var + LN_EPS) * g + b


def _rms_norm(x, g):
    return x * lax.rsqrt(jnp.mean(x * x, axis=-1, keepdims=True) + RMS_EPS) * g


def _gelu(x):
    return 0.5 * x * (1.0 + jnp.tanh(math.sqrt(2.0 / math.pi) * (x + 0.044715 * (x * x * x))))


def _log_sigmoid(x):
    return -(jnp.maximum(-x, 0.0) + jnp.log1p(jnp.exp(-jnp.abs(x))))


def _shift_rows(x, d, row):
    return jnp.where(row >= d, pltpu.roll(x, d, axis=0), 0.0)


def _ffn_ln_kernel(x_ref, wg_ref, wu_ref, wd_ref, g_ref, b_ref, o_ref):
    x = x_ref[...]
    xb = x.astype(BF16)
    gate = jnp.dot(xb, wg_ref[...], preferred_element_type=F32)
    up = jnp.dot(xb, wu_ref[...], preferred_element_type=F32)
    h = (gate * jax.nn.sigmoid(gate) * up).astype(BF16)
    y = jnp.dot(h, wd_ref[...], preferred_element_type=F32)
    o_ref[...] = _layer_norm(ALPHA * x + 0.5 * y, g_ref[...], b_ref[...])


def _ffn_ln(x2d, wg, wu, wd, g, b, tm):
    n = x2d.shape[0]
    row = pl.BlockSpec((tm, D_MODEL), lambda i: (i, 0))
    return pl.pallas_call(
        _ffn_ln_kernel,
        grid=(n // tm,),
        in_specs=[row, _const_spec((D_MODEL, D_FF)), _const_spec((D_MODEL, D_FF)),
                  _const_spec((D_FF, D_MODEL)), _const_spec((1, D_MODEL)), _const_spec((1, D_MODEL))],
        out_specs=row,
        out_shape=jax.ShapeDtypeStruct((n, D_MODEL), F32),
        compiler_params=_cparams("parallel"),
        name="ffn_ln",
    )(x2d, wg, wu, wd, g, b)


def _rglru_tile(ax, ag, cw_ref, cb_ref, wg_ref, bg_ref, sp_ref, gn_ref, prev8_sc, hcarry_sc, hs_sc):
    tm = ax.shape[0]
    groups = tm // 8
    grouped = lambda v: v.reshape(groups, 8, D_A)
    sub = lax.broadcasted_iota(jnp.int32, (groups, 8, D_A), 1)
    ax3 = grouped(ax)
    prev8 = prev8_sc[...]
    xa = cb_ref[...] + cw_ref[CONV_WIDTH - 1:CONV_WIDTH, :] * ax3
    for k in range(1, CONV_WIDTH):
        rolled = pltpu.roll(ax3, k, axis=1)
        before = jnp.concatenate([pltpu.roll(prev8, k, axis=0)[None], rolled[:groups - 1]], axis=0)
        xa = xa + cw_ref[CONV_WIDTH - 1 - k:CONV_WIDTH - k, :] * jnp.where(sub >= k, rolled, before)
    prev8_sc[...] = ax3[groups - 1]
    xa = xa.reshape(tm, D_A)

    gates = jnp.dot(xa.astype(BF16), wg_ref[...], preferred_element_type=F32) + bg_ref[...]
    r = jax.nn.sigmoid(gates[:, :D_A])
    i = jax.nn.sigmoid(gates[:, D_A:])
    log_a = (-RG_C) * r * sp_ref[...]
    a = jnp.exp(log_a)
    y = -jnp.tanh(log_a) * (1.0 + a * a)
    b = jnp.where(y > 0.0, y * lax.rsqrt(y), 0.0) * (i * xa)

    a3, b3 = grouped(a), grouped(b)
    for d in (1, 2, 4):
        valid = sub >= d
        b3 = jnp.where(valid, a3 * pltpu.roll(b3, d, axis=1) + b3, b3)
        a3 = jnp.where(valid, a3 * pltpu.roll(a3, d, axis=1), a3)
    last = hcarry_sc[...]
    for g in range(groups):
        hg = b3[g] + a3[g] * last
        hs_sc[8 * g:8 * g + 8, :] = hg
        last = hg[7:8, :]
    hcarry_sc[...] = last
    return _rms_norm(_gelu(ag) * hs_sc[...], gn_ref[...])


def _in_proj_kernel(x_ref, w_ref, bf_ref, fmask_ref, ones_ref, cw_ref, cb_ref, wg_ref, bg_ref, sp_ref, gn_ref,
                    oa_ref, qa_ref, ka_ref, v_ref, cu_ref, base_ref,
                    carry_sc, prev8_sc, hcarry_sc, hs_sc, cu_sc):
    tm = x_ref.shape[0]

    @pl.when(pl.program_id(1) == 0)
    def _():
        carry_sc[...] = jnp.zeros_like(carry_sc)
        prev8_sc[...] = jnp.zeros_like(prev8_sc)
        hcarry_sc[...] = jnp.zeros_like(hcarry_sc)

    xb = x_ref[...].astype(BF16)
    z = jnp.dot(xb, w_ref[...], preferred_element_type=F32)
    o = 2 * D_A
    v_ref[...] = z[:, o + 2 * D_B:o + 3 * D_B].astype(BF16)

    oa_ref[...] = _rglru_tile(z[:, 0:D_A], z[:, D_A:2 * D_A], cw_ref, cb_ref, wg_ref, bg_ref, sp_ref, gn_ref,
                              prev8_sc, hcarry_sc, hs_sc).astype(BF16)

    cu = z[:, o + 3 * D_B:o + 3 * D_B + D_C]
    cu_sc[0] = cu[:, :128]
    cu_sc[1] = cu[:, 128:]
    cu_ref[...] = jnp.concatenate([cu_sc[j, pl.ds(t, tm // S5_CHUNK, stride=S5_CHUNK), :]
                                   for t in range(S5_CHUNK) for j in range(2)], axis=1)

    lf = _log_sigmoid(z[:, N_MAIN - 128:] + bf_ref[...]) * fmask_ref[...] * LOG2E
    row = lax.broadcasted_iota(jnp.int32, (tm, 128), 0)
    lane = lax.broadcasted_iota(jnp.int32, (tm, 128), 1)
    c = lf
    d = 1
    while d < tm:
        c = c + _shift_rows(c, d, row)
        d *= 2
    carry = carry_sc[...]
    rows = [jnp.broadcast_to(carry[:, F_LANE0 + h:F_LANE0 + h + 1], (1, 128)) for h in range(B_HEADS)]
    pad = jnp.zeros((6, 128), F32)
    base_ref[0, 0] = jnp.stack([jnp.concatenate([rows[2 * p], rows[2 * p + 1], pad], axis=0)
                                for p in range(HEAD_PAIRS)])
    carry_sc[...] = carry + c[tm - 1:tm, :]

    hi = c.astype(BF16).astype(F32)
    r1 = c - hi
    mid = r1.astype(BF16).astype(F32)
    lo = (r1 - mid).astype(BF16).astype(F32)
    pieces = jnp.where(lane < F_LANE0 + 8, hi, jnp.where(lane < F_LANE0 + 16, mid, lo))
    for p in range(HEAD_PAIRS):
        qt = z[:, o + 128 * p:o + 128 * (p + 1)]
        kt = z[:, o + D_B + 128 * p:o + D_B + 128 * (p + 1)]
        for e in range(2):
            h = 2 * p + e
            qh = pltpu.roll(qt, B_HEAD_DIM, axis=1) if e else qt
            kh = pltpu.roll(kt, B_HEAD_DIM, axis=1) if e else kt
            qa_ref[0, h] = jnp.where(lane < B_HEAD_DIM, qh * QSCALE, ones_ref[h:h + 1, :]).astype(BF16)
            ka_ref[0, h] = jnp.where(lane < B_HEAD_DIM, kh, -pieces).astype(BF16)


def _in_proj(x2d, w_main, bf_row, fmask, ones, cw, cb, wg, bg, sp, gn, bsz, seq, tm):
    ns = seq // tm
    n = bsz * seq
    tok = lambda rows, width: pl.BlockSpec((rows, width), lambda b, s: (b * ns + s, 0))
    heads = pl.BlockSpec((1, B_HEADS, tm, 128), lambda b, s: (b, 0, s, 0))
    return pl.pallas_call(
        _in_proj_kernel,
        grid=(bsz, ns),
        in_specs=[tok(tm, D_MODEL), _const_spec((D_MODEL, N_MAIN)), _const_spec((1, 128)), _const_spec((1, 128)),
                  _const_spec((B_HEADS, 128)), _const_spec((CONV_WIDTH, D_A)), _const_spec((1, D_A)),
                  _const_spec((D_A, 2 * D_A)), _const_spec((1, 2 * D_A)), _const_spec((1, D_A)),
                  _const_spec((1, D_A))],
        out_specs=[tok(tm, D_A), heads, heads, tok(tm, D_B), tok(tm // S5_CHUNK, S5_CHUNK * D_C),
                   pl.BlockSpec((1, 1, HEAD_PAIRS, 8, 128), lambda b, s: (b, s, 0, 0, 0))],
        out_shape=[jax.ShapeDtypeStruct((n, D_A), BF16),
                   jax.ShapeDtypeStruct((bsz, B_HEADS, seq, 128), BF16),
                   jax.ShapeDtypeStruct((bsz, B_HEADS, seq, 128), BF16),
                   jax.ShapeDtypeStruct((n, D_B), BF16),
                   jax.ShapeDtypeStruct((n // S5_CHUNK, S5_CHUNK * D_C), F32),
                   jax.ShapeDtypeStruct((bsz, ns, HEAD_PAIRS, 8, 128), F32)],
        scratch_shapes=[pltpu.VMEM((1, 128), F32), pltpu.VMEM((8, D_A), F32), pltpu.VMEM((1, D_A), F32),
                        pltpu.VMEM((tm, D_A), F32), pltpu.VMEM((2, tm, 128), F32)],
        compiler_params=_cparams("parallel", "arbitrary"),
        name="in_proj",
    )(x2d, w_main, bf_row, fmask, ones, cw, cb, wg, bg, sp, gn)


def _fox_kernel(qa_ref, ka_ref, v_ref, base_ref, o_ref, vaug_sc, acc_sc, m_sc, *, tq):
    qi = pl.program_id(2)
    seq = v_ref.shape[1]
    lane = lax.broadcasted_iota(jnp.int32, (ROWS, 128), 1)
    sub = lax.broadcasted_iota(jnp.int32, (ROWS, 128), 0)

    @pl.when(qi == 0)
    def _():
        lane_v = lax.broadcasted_iota(jnp.int32, (tq, 128), 1)
        for c in range(seq // tq):
            vc = v_ref[0, c * tq:(c + 1) * tq, :].astype(F32)
            vaug_sc[0, c * tq:(c + 1) * tq, :] = jnp.concatenate(
                [jnp.where(lane_v < B_HEAD_DIM, vc, 0.0), jnp.where(lane_v == 0, 1.0, 0.0)], axis=1).astype(BF16)
            vaug_sc[1, c * tq:(c + 1) * tq, :] = jnp.concatenate(
                [jnp.where(lane_v >= B_HEAD_DIM, vc, 0.0), jnp.where(lane_v == 1, 1.0, 0.0)], axis=1).astype(BF16)

    acc_sc[...] = jnp.zeros_like(acc_sc)
    m_sc[...] = jnp.full_like(m_sc, NEG)
    base_q = base_ref[0, qi, 0]

    def block(k0, base_k, diag):
        deltas = [base_q[j:j + 1, :] - base_k[j:j + 1, :] for j in range(2)]
        for r in range(tq // ROWS):
            rows = slice(r * ROWS, (r + 1) * ROWS)
            n = (r + 1) * ROWS if diag else tq
            ps, alphas = [], []
            for j in range(2):
                s = lax.dot_general(qa_ref[0, j, rows, :], ka_ref[0, j, pl.ds(k0, n), :],
                                    (((1,), (1,)), ((), ())), preferred_element_type=F32)
                tiles = [s[:, c * 128:(c + 1) * 128] for c in range(n // 128)]
                if diag:
                    tiles[-1] = jnp.where(lane <= sub, tiles[-1], NEG)
                tmax = tiles[0]
                for t in tiles[1:]:
                    tmax = jnp.maximum(tmax, t)
                m_cur = jnp.max(tmax, axis=-1, keepdims=True) + deltas[j]
                m_prev = m_sc[j, rows, :]
                m_new = jnp.maximum(m_prev, m_cur)
                alphas.append(jnp.exp2(m_prev - m_new))
                m_sc[j, rows, :] = m_new
                mt = m_new - deltas[j]
                ps.append(jnp.concatenate([jnp.exp2(t - mt) for t in tiles], axis=1).astype(BF16))
            vk = jnp.concatenate([vaug_sc[0, pl.ds(k0, n), :], vaug_sc[1, pl.ds(k0, n), :]], axis=0)
            pv = jnp.dot(jnp.concatenate(ps, axis=1), vk, preferred_element_type=F32)
            a = jnp.concatenate([jnp.where(lane < B_HEAD_DIM, alphas[0], alphas[1]),
                                 jnp.where(lane == 0, alphas[0], alphas[1])], axis=1)
            acc_sc[rows, :] = a * acc_sc[rows, :] + pv

    def past_block(ki):
        block(pl.multiple_of(ki * tq, tq), base_ref[0, ki, 0], False)

    def body(i, carry):
        for u in range(4):
            past_block(4 * i + u)
        return carry

    lax.fori_loop(0, qi // 4, body, 0)
    rem0 = (qi // 4) * 4

    @pl.when(qi % 4 >= 2)
    def _():
        past_block(rem0)
        past_block(rem0 + 1)

    @pl.when(qi % 2 == 1)
    def _():
        past_block(qi - 1)

    block(pl.multiple_of(qi * tq, tq), base_q, True)

    acc = acc_sc[...]
    lane_o = lax.broadcasted_iota(jnp.int32, (tq, 128), 1)
    o_ref[0] = acc[:, :128] / jnp.where(lane_o < B_HEAD_DIM, acc[:, 128:129], acc[:, 129:130])


def _fox(qa, ka, v3, base, tq):
    bsz, seq, _ = v3.shape
    ns = seq // tq
    return pl.pallas_call(
        functools.partial(_fox_kernel, tq=tq),
        grid=(bsz, HEAD_PAIRS, ns),
        in_specs=[pl.BlockSpec((1, 2, tq, 128), lambda b, h, i: (b, h, i, 0)),
                  pl.BlockSpec((1, 2, seq, 128), lambda b, h, i: (b, h, 0, 0)),
                  pl.BlockSpec((1, seq, 128), lambda b, h, i: (b, 0, h)),
                  pl.BlockSpec((1, ns, 1, 8, 128), lambda b, h, i: (b, 0, h, 0, 0))],
        out_specs=pl.BlockSpec((1, tq, 128), lambda b, h, i: (b, i, h)),
        out_shape=jax.ShapeDtypeStruct((bsz, seq, D_B), F32),
        scratch_shapes=[pltpu.VMEM((2, seq, 256), BF16), pltpu.VMEM((tq, 256), F32),
                        pltpu.VMEM((2, tq, 128), F32)],
        compiler_params=_cparams("parallel", "parallel", "arbitrary"),
        name="fox_attention",
    )(qa, ka, v3, base)


def _s5_kernel(u_ref, m_ref, bm_ref, cm_ref, ap_ref, d_ref, wglu_ref, gn_ref, o_ref, *, levels):
    u = u_ref[...]
    rows = u.shape[0]
    ub = u.astype(BF16)
    row = lax.broadcasted_iota(jnp.int32, (rows, S5_HSTATE), 0)
    y_halves = []
    for h in range(2):
        uh = jnp.concatenate([ub[:, t * D_C + h * 128: t * D_C + (h + 1) * 128] for t in range(S5_CHUNK)],
                             axis=1)
        y = jnp.dot(uh, m_ref[h], preferred_element_type=F32)
        st = jnp.dot(uh, bm_ref[h], preferred_element_type=F32)
        hre = st[:, :S5_HSTATE]
        him = st[:, S5_HSTATE:]
        for lv in range(levels):
            d = 1 << lv
            ar = ap_ref[lv:lv + 1, (2 * h) * S5_HSTATE:(2 * h + 1) * S5_HSTATE]
            ai = ap_ref[lv:lv + 1, (2 * h + 1) * S5_HSTATE:(2 * h + 2) * S5_HSTATE]
            sre = _shift_rows(hre, d, row)
            sim = _shift_rows(him, d, row)
            hre, him = hre + (ar * sre - ai * sim), him + (ar * sim + ai * sre)
        hprev = jnp.concatenate([_shift_rows(hre, 1, row), _shift_rows(him, 1, row)], axis=1)
        y = y + jnp.dot(hprev.astype(BF16), cm_ref[h], preferred_element_type=F32)
        y_halves.append(y)
    outs = []
    for t in range(S5_CHUNK):
        yt = jnp.concatenate([y_halves[0][:, t * 128:(t + 1) * 128], y_halves[1][:, t * 128:(t + 1) * 128]], axis=1)
        yt = _gelu(yt + d_ref[...] * u[:, t * D_C:(t + 1) * D_C])
        gl = jnp.dot(yt.astype(BF16), wglu_ref[...], preferred_element_type=F32)
        outs.append(_rms_norm(yt * jax.nn.sigmoid(gl), gn_ref[...]))
    o_ref[...] = jnp.concatenate(outs, axis=1)


def _s5(u2, m, bm, cm, ap, d, wglu, gn, rows_per_seq):
    n, width = u2.shape
    levels = ap.shape[0]
    blk = pl.BlockSpec((rows_per_seq, width), lambda b: (b, 0))
    return pl.pallas_call(
        functools.partial(_s5_kernel, levels=levels),
        grid=(n // rows_per_seq,),
        in_specs=[blk, _const_spec(m.shape), _const_spec(bm.shape), _const_spec(cm.shape),
                  _const_spec(ap.shape), _const_spec((1, D_C)), _const_spec((D_C, D_C)), _const_spec((1, D_C))],
        out_specs=blk,
        out_shape=jax.ShapeDtypeStruct(u2.shape, F32),
        compiler_params=_cparams("parallel"),
        name="s5",
    )(u2, m, bm, cm, ap, d, wglu, gn)


def _s5_prepare(a_re, a_im, log_dt, b_re, b_im, c_re, c_im, n_chunks):
    hi = lax.Precision.HIGHEST
    T = S5_CHUNK
    dt = jnp.exp(log_dt)[:, None]
    lr, li = a_re * dt, a_im * dt

    def cpow(tau):
        mag = jnp.exp(lr[None] * tau[:, None, None])
        ang = li[None] * tau[:, None, None]
        return mag * jnp.cos(ang), mag * jnp.sin(ang)

    ar, ai = cpow(jnp.arange(T + 1, dtype=F32))
    nr, ni = ar[1] - 1.0, ai[1]
    den = a_re * a_re + a_im * a_im
    fr, fi = (nr * a_re + ni * a_im) / den, (ni * a_re - nr * a_im) / den
    bbr = fr[:, :, None] * b_re - fi[:, :, None] * b_im
    bbi = fr[:, :, None] * b_im + fi[:, :, None] * b_re
    abr = ar[:, :, :, None] * bbr[None] - ai[:, :, :, None] * bbi[None]
    abi = ar[:, :, :, None] * bbi[None] + ai[:, :, :, None] * bbr[None]
    kern = (jnp.einsum('gop,tgpi->tgio', c_re, abr[:T], precision=hi)
            - jnp.einsum('gop,tgpi->tgio', c_im, abi[:T], precision=hi))
    tin = jnp.arange(T)[:, None]
    tout = jnp.arange(T)[None, :]
    lag = tout - tin
    ktt = jnp.where((lag >= 0)[:, :, None, None, None], kern[jnp.clip(lag, 0, T - 1)], 0.0)
    eye = jnp.eye(S5_HALF, dtype=F32)
    rev = T - 1 - jnp.arange(T)
    sin_r, sin_i = abr[rev], abi[rev]
    so_r = c_re[None] * ar[1:, :, None, :] - c_im[None] * ai[1:, :, None, :]
    so_i = c_re[None] * ai[1:, :, None, :] + c_im[None] * ar[1:, :, None, :]
    ms, bms, cms = [], [], []
    for h in range(2):
        gs = slice(h * S5_HALF, (h + 1) * S5_HALF)
        mh = jnp.einsum('abgio,gk->agibko', ktt[:, :, gs], eye)
        ms.append(mh.reshape(S5_HCOLS, S5_HCOLS))
        parts = [jnp.einsum('agpi,gk->agikp', s[:, gs], eye).reshape(S5_HCOLS, S5_HSTATE) for s in (sin_r, sin_i)]
        bms.append(jnp.concatenate(parts, axis=1))
        parts = [jnp.einsum('bgop,gk->gpbko', s[:, gs], eye).reshape(S5_HSTATE, S5_HCOLS) for s in (so_r, -so_i)]
        cms.append(jnp.concatenate(parts, axis=0))
    levels = max(1, int(math.ceil(math.log2(n_chunks))))
    lev_r, lev_i = cpow(jnp.asarray((T * (2 ** np.arange(levels))).astype(np.float32)))
    aps = []
    for h in range(2):
        gs = slice(h * S5_HALF, (h + 1) * S5_HALF)
        aps += [lev_r[:, gs].reshape(levels, S5_HSTATE), lev_i[:, gs].reshape(levels, S5_HSTATE)]
    return (jnp.stack(ms).astype(BF16), jnp.stack(bms).astype(BF16), jnp.stack(cms).astype(BF16),
            jnp.concatenate(aps, axis=1))


def _out_proj_kernel(oa_ref, ob_ref, oc_ref, x_ref, gb_ref, w_ref, g_ref, b_ref, o_ref, oc_sc):
    tm = x_ref.shape[0]
    for t in range(S5_CHUNK):
        for j in range(2):
            oc_sc[j, pl.ds(t, tm // S5_CHUNK, stride=S5_CHUNK), :] = oc_ref[:, t * D_C + j * 128:t * D_C + (j + 1) * 128]
    nb = _rms_norm(ob_ref[...], gb_ref[...]).astype(BF16)
    o = jnp.concatenate([oa_ref[...], nb, oc_sc[0].astype(BF16), oc_sc[1].astype(BF16)], axis=1)
    mix = jnp.dot(o, w_ref[...], preferred_element_type=F32)
    o_ref[...] = _layer_norm(ALPHA * x_ref[...] + mix, g_ref[...], b_ref[...])


def _out_proj(oa2, ob2, oc2, x2d, gb, w, g, b, tm):
    tok = lambda rows, width: pl.BlockSpec((rows, width), lambda i: (i, 0))
    return pl.pallas_call(
        _out_proj_kernel,
        grid=(x2d.shape[0] // tm,),
        in_specs=[tok(tm, D_A), tok(tm, D_B), tok(tm // S5_CHUNK, S5_CHUNK * D_C), tok(tm, D_MODEL),
                  _const_spec((1, D_B)), _const_spec((D_MODEL, D_MODEL)), _const_spec((1, D_MODEL)),
                  _const_spec((1, D_MODEL))],
        out_specs=tok(tm, D_MODEL),
        out_shape=jax.ShapeDtypeStruct(x2d.shape, F32),
        scratch_shapes=[pltpu.VMEM((2, tm, 128), F32)],
        compiler_params=_cparams("parallel"),
        name="out_proj_ln",
    )(oa2, ob2, oc2, x2d, gb, w, g, b)


def _block_diag(w):
    eye = jnp.eye(A_BLOCKS, dtype=w.dtype)
    return jnp.einsum('hij,hk->hikj', w, eye).reshape(D_A, D_A)


def _pick(n, pref):
    t = min(n, pref)
    while n % t:
        t //= 2
    return t


def kernel(x, ffn1_w_gate, ffn1_w_up, ffn1_w_down, ln1_g, ln1_b, w_in, conv_w, conv_b, rg_w_a, rg_b_a, rg_w_x, rg_b_x, rg_lambda, fox_b_f, s5_a_re, s5_a_im, s5_log_dt, s5_b_re, s5_b_im, s5_c_re, s5_c_im, s5_d, s5_w_glu, mix_norm_g, w_out, ln2_g, ln2_b, ffn2_w_gate, ffn2_w_up, ffn2_w_down, ln3_g, ln3_b):
    bsz, seq, _ = x.shape
    n = bsz * seq
    tm = _pick(seq, 512)
    n_chunks = seq // S5_CHUNK
    row = lambda v: v.reshape(1, -1)
    h = x.reshape(n, D_MODEL)
    cuts = np.cumsum([D_A, D_A, D_B, D_B, D_B, B_HEADS])
    for l in range(DEPTH):
        h = _ffn_ln(h, ffn1_w_gate[l].astype(BF16), ffn1_w_up[l].astype(BF16), ffn1_w_down[l].astype(BF16),
                    row(ln1_g[l]), row(ln1_b[l]), tm)

        wl = w_in[l]
        fcols = jnp.zeros((D_MODEL, 128), F32)
        bf_row = jnp.zeros((128,), F32)
        fmask = np.zeros((1, 128), np.float32)
        ones = np.zeros((B_HEADS, 128), np.float32)
        for piece in range(3):
            lo_ = F_LANE0 + 8 * piece
            fcols = fcols.at[:, lo_:lo_ + B_HEADS].set(wl[:, cuts[4]:cuts[5]])
            bf_row = bf_row.at[lo_:lo_ + B_HEADS].set(fox_b_f[l])
            fmask[0, lo_:lo_ + B_HEADS] = 1.0
            ones[np.arange(B_HEADS), lo_ + np.arange(B_HEADS)] = 1.0
        w_main = jnp.concatenate([wl[:, :cuts[4]], wl[:, cuts[5]:], fcols], axis=1).astype(BF16)
        wgates = jnp.concatenate([_block_diag(rg_w_a[l]), _block_diag(rg_w_x[l])], axis=1).astype(BF16)
        bgates = jnp.concatenate([rg_b_a[l], rg_b_x[l]]).reshape(1, 2 * D_A)
        g_a, g_b, g_c = jnp.split(mix_norm_g[l], [D_A, D_A + D_B])
        oa, qa, ka, v, cu, base = _in_proj(h, w_main, bf_row.reshape(1, 128), jnp.asarray(fmask), jnp.asarray(ones),
                                           conv_w[l], row(conv_b[l]), wgates, bgates,
                                           row(jax.nn.softplus(-rg_lambda[l])), row(g_a), bsz, seq, tm)

        ob = _fox(qa, ka, v.reshape(bsz, seq, D_B), base, tm)

        m, bm, cm, ap = _s5_prepare(s5_a_re[l], s5_a_im[l], s5_log_dt[l], s5_b_re[l], s5_b_im[l],
                                    s5_c_re[l], s5_c_im[l], n_chunks)
        oc = _s5(cu, m, bm, cm, ap, row(s5_d[l]), s5_w_glu[l].astype(BF16), row(g_c), n_chunks)

        h = _out_proj(oa, ob.reshape(n, D_B), oc, h, row(g_b),
                      w_out[l].astype(BF16), row(ln2_g[l]), row(ln2_b[l]), tm)
        h = _ffn_ln(h, ffn2_w_gate[l].astype(BF16), ffn2_w_up[l].astype(BF16), ffn2_w_down[l].astype(BF16),
                    row(ln3_g[l]), row(ln3_b[l]), tm)
    return h.reshape(bsz, seq, D_MODEL)
```

```python
import functools
import math

import numpy as np
import jax
import jax.numpy as jnp
from jax import lax
from jax.experimental import pallas as pl
from jax.experimental.pallas import tpu as pltpu

F32 = jnp.float32
BF16 = jnp.bfloat16

D_MODEL = 1024
D_FF = 2816
D_A = 384
A_BLOCKS = 6
A_BLOCK_DIM = 64
CONV_WIDTH = 4
RG_C = 8.0
D_B = 384
B_HEADS = 6
B_HEAD_DIM = 64
HEAD_PAIRS = B_HEADS // 2
D_C = 256
C_GROUP = 16
C_GROUPS = 16
C_STATE = 64
N_MAIN = 2 * D_A + 3 * D_B + D_C + 128
F_LANE0 = 64
LOG2E = math.log2(math.e)
QSCALE = LOG2E * B_HEAD_DIM ** -0.5
NEG = -1e30
ROWS = 128
DEPTH = 2
ALPHA = (2 * DEPTH) ** 0.25
LN_EPS = 1e-5
RMS_EPS = 1e-6

S5_CHUNK = 8
S5_HALF = C_GROUPS // 2
S5_HCOLS = S5_CHUNK * S5_HALF * C_GROUP
S5_HSTATE = S5_HALF * C_STATE

VMEM_LIMIT = 56 * 1024 * 1024


def _cparams(*sem):
    return pltpu.CompilerParams(dimension_semantics=sem, vmem_limit_bytes=VMEM_LIMIT)


def _const_spec(shape):
    nd = len(shape)
    return pl.BlockSpec(shape, lambda *_: (0,) * nd, pipeline_mode=pl.Buffered(1))


def _layer_norm(r, g, b):
    mu = jnp.mean(r, axis=-1, keepdims=True)
    d = r - mu
    var = jnp.mean(d * d, axis=-1, keepdims=True)
    return d * lax.rsqrt(var + LN_EPS) * g + b


def _rms_norm(x, g):
    return x * lax.rsqrt(jnp.mean(x * x, axis=-1, keepdims=True) + RMS_EPS) * g


def _gelu(x):
    return 0.5 * x * (1.0 + jnp.tanh(math.sqrt(2.0 / math.pi) * (x + 0.044715 * (x * x * x))))


def _log_sigmoid(x):
    return -(jnp.maximum(-x, 0.0) + jnp.log1p(jnp.exp(-jnp.abs(x))))


def _shift_rows(x, d):
    rows, cols = x.shape
    groups = rows // 8
    x3 = x.reshape(groups, 8, cols)
    if d % 8 == 0:
        return jnp.concatenate([jnp.zeros((d // 8, 8, cols), x.dtype), x3[:groups - d // 8]], axis=0).reshape(rows, cols)
    sub = lax.broadcasted_iota(jnp.int32, (groups, 8, cols), 1)
    rolled = pltpu.roll(x3, d, axis=1)
    before = jnp.concatenate([jnp.zeros((1, 8, cols), x.dtype), rolled[:groups - 1]], axis=0)
    return jnp.where(sub >= d, rolled, before).reshape(rows, cols)


def _ffn_ln_tile(x, wg_ref, wu_ref, wd_ref, g_ref, b_ref):
    xb = x.astype(BF16)
    gate = jnp.dot(xb, wg_ref[...], preferred_element_type=F32)
    up = jnp.dot(xb, wu_ref[...], preferred_element_type=F32)
    h = (gate * jax.nn.sigmoid(gate) * up).astype(BF16)
    y = jnp.dot(h, wd_ref[...], preferred_element_type=F32)
    return _layer_norm(ALPHA * x + 0.5 * y, g_ref[...], b_ref[...])


def _ffn_ln_kernel(x_ref, wg_ref, wu_ref, wd_ref, g_ref, b_ref, o_ref):
    o_ref[...] = _ffn_ln_tile(x_ref[...], wg_ref, wu_ref, wd_ref, g_ref, b_ref)


def _ffn_ln(x2d, wg, wu, wd, g, b, tm):
    n = x2d.shape[0]
    row = pl.BlockSpec((tm, D_MODEL), lambda i: (i, 0))
    return pl.pallas_call(
        _ffn_ln_kernel,
        grid=(n // tm,),
        in_specs=[row, _const_spec((D_MODEL, D_FF)), _const_spec((D_MODEL, D_FF)),
                  _const_spec((D_FF, D_MODEL)), _const_spec((1, D_MODEL)), _const_spec((1, D_MODEL))],
        out_specs=row,
        out_shape=jax.ShapeDtypeStruct((n, D_MODEL), F32),
        compiler_params=_cparams("parallel"),
        name="ffn_ln",
    )(x2d, wg, wu, wd, g, b)


def _rglru_tile(ax, ag, cw_ref, cb_ref, wg_ref, bg_ref, sp_ref, gn_ref, prev8_sc, hcarry_sc, hs_sc):
    tm = ax.shape[0]
    groups = tm // 8
    grouped = lambda v: v.reshape(groups, 8, D_A)
    sub = lax.broadcasted_iota(jnp.int32, (groups, 8, D_A), 1)
    ax3 = grouped(ax)
    prev8 = prev8_sc[...]
    xa = cb_ref[...] + cw_ref[CONV_WIDTH - 1:CONV_WIDTH, :] * ax3
    for k in range(1, CONV_WIDTH):
        rolled = pltpu.roll(ax3, k, axis=1)
        before = jnp.concatenate([pltpu.roll(prev8, k, axis=0)[None], rolled[:groups - 1]], axis=0)
        xa = xa + cw_ref[CONV_WIDTH - 1 - k:CONV_WIDTH - k, :] * jnp.where(sub >= k, rolled, before)
    prev8_sc[...] = ax3[groups - 1]
    xa = xa.reshape(tm, D_A)

    gates = jnp.dot(xa.astype(BF16), wg_ref[...], preferred_element_type=F32) + bg_ref[...]
    r = jax.nn.sigmoid(gates[:, :D_A])
    i = jax.nn.sigmoid(gates[:, D_A:])
    log_a = (-RG_C) * r * sp_ref[...]
    a = jnp.exp(log_a)
    y = -jnp.tanh(log_a) * (1.0 + a * a)
    b = jnp.where(y > 0.0, y * lax.rsqrt(y), 0.0) * (i * xa)

    a3, b3 = grouped(a), grouped(b)
    for d in (1, 2, 4):
        valid = sub >= d
        b3 = jnp.where(valid, a3 * pltpu.roll(b3, d, axis=1) + b3, b3)
        a3 = jnp.where(valid, a3 * pltpu.roll(a3, d, axis=1), a3)
    last = hcarry_sc[...]
    for g in range(groups):
        hg = b3[g] + a3[g] * last
        hs_sc[8 * g:8 * g + 8, :] = hg
        last = hg[7:8, :]
    hcarry_sc[...] = last
    return _rms_norm(_gelu(ag) * hs_sc[...], gn_ref[...])


def _in_proj_kernel(x_ref, w_ref, bf_ref, fmask_ref, ones_ref, cw_ref, cb_ref, wg_ref, bg_ref, sp_ref, gn_ref,
                    oa_ref, qa_ref, ka_ref, v_ref, cu_ref, base_ref,
                    carry_sc, prev8_sc, hcarry_sc, hs_sc, cu_sc):
    tm = x_ref.shape[0]

    @pl.when(pl.program_id(1) == 0)
    def _():
        carry_sc[...] = jnp.zeros_like(carry_sc)
        prev8_sc[...] = jnp.zeros_like(prev8_sc)
        hcarry_sc[...] = jnp.zeros_like(hcarry_sc)

    xb = x_ref[...].astype(BF16)
    z = jnp.dot(xb, w_ref[...], preferred_element_type=F32)
    o = 2 * D_A
    v_ref[...] = z[:, o + 2 * D_B:o + 3 * D_B].astype(BF16)

    oa_ref[...] = _rglru_tile(z[:, 0:D_A], z[:, D_A:2 * D_A], cw_ref, cb_ref, wg_ref, bg_ref, sp_ref, gn_ref,
                              prev8_sc, hcarry_sc, hs_sc).astype(BF16)

    cu = z[:, o + 3 * D_B:o + 3 * D_B + D_C]
    cu_sc[0] = cu[:, :128]
    cu_sc[1] = cu[:, 128:]
    cu_ref[...] = jnp.concatenate([cu_sc[j, pl.ds(t, tm // S5_CHUNK, stride=S5_CHUNK), :]
                                   for t in range(S5_CHUNK) for j in range(2)], axis=1)

    lf = _log_sigmoid(z[:, N_MAIN - 128:] + bf_ref[...]) * fmask_ref[...] * LOG2E
    lane = lax.broadcasted_iota(jnp.int32, (tm, 128), 1)
    c = lf
    d = 1
    while d < tm:
        c = c + _shift_rows(c, d)
        d *= 2
    carry = carry_sc[...]
    rows = [jnp.broadcast_to(carry[:, F_LANE0 + h:F_LANE0 + h + 1], (1, 128)) for h in range(B_HEADS)]
    pad = jnp.zeros((6, 128), F32)
    base_ref[0, 0] = jnp.stack([jnp.concatenate([rows[2 * p], rows[2 * p + 1], pad], axis=0)
                                for p in range(HEAD_PAIRS)])
    carry_sc[...] = carry + c[tm - 1:tm, :]

    hi = c.astype(BF16).astype(F32)
    r1 = c - hi
    mid = r1.astype(BF16).astype(F32)
    lo = (r1 - mid).astype(BF16).astype(F32)
    pieces = jnp.where(lane < F_LANE0 + 8, hi, jnp.where(lane < F_LANE0 + 16, mid, lo))
    for p in range(HEAD_PAIRS):
        qt = z[:, o + 128 * p:o + 128 * (p + 1)]
        kt = z[:, o + D_B + 128 * p:o + D_B + 128 * (p + 1)]
        for e in range(2):
            h = 2 * p + e
            qh = pltpu.roll(qt, B_HEAD_DIM, axis=1) if e else qt
            kh = pltpu.roll(kt, B_HEAD_DIM, axis=1) if e else kt
            qa_ref[0, h] = jnp.where(lane < B_HEAD_DIM, qh * QSCALE, ones_ref[h:h + 1, :]).astype(BF16)
            ka_ref[0, h] = jnp.where(lane < B_HEAD_DIM, kh, -pieces).astype(BF16)


def _in_proj(x2d, w_main, bf_row, fmask, ones, cw, cb, wg, bg, sp, gn, bsz, seq, tm):
    ns = seq // tm
    n = bsz * seq
    tok = lambda rows, width: pl.BlockSpec((rows, width), lambda b, s: (b * ns + s, 0))
    heads = pl.BlockSpec((1, B_HEADS, tm, 128), lambda b, s: (b, 0, s, 0))
    return pl.pallas_call(
        _in_proj_kernel,
        grid=(bsz, ns),
        in_specs=[tok(tm, D_MODEL), _const_spec((D_MODEL, N_MAIN)), _const_spec((1, 128)), _const_spec((1, 128)),
                  _const_spec((B_HEADS, 128)), _const_spec((CONV_WIDTH, D_A)), _const_spec((1, D_A)),
                  _const_spec((D_A, 2 * D_A)), _const_spec((1, 2 * D_A)), _const_spec((1, D_A)),
                  _const_spec((1, D_A))],
        out_specs=[tok(tm, D_A), heads, heads, tok(tm, D_B), tok(tm // S5_CHUNK, S5_CHUNK * D_C),
                   pl.BlockSpec((1, 1, HEAD_PAIRS, 8, 128), lambda b, s: (b, s, 0, 0, 0))],
        out_shape=[jax.ShapeDtypeStruct((n, D_A), BF16),
                   jax.ShapeDtypeStruct((bsz, B_HEADS, seq, 128), BF16),
                   jax.ShapeDtypeStruct((bsz, B_HEADS, seq, 128), BF16),
                   jax.ShapeDtypeStruct((n, D_B), BF16),
                   jax.ShapeDtypeStruct((n // S5_CHUNK, S5_CHUNK * D_C), F32),
                   jax.ShapeDtypeStruct((bsz, ns, HEAD_PAIRS, 8, 128), F32)],
        scratch_shapes=[pltpu.VMEM((1, 128), F32), pltpu.VMEM((8, D_A), F32), pltpu.VMEM((1, D_A), F32),
                        pltpu.VMEM((tm, D_A), F32), pltpu.VMEM((2, tm, 128), F32)],
        compiler_params=_cparams("parallel", "arbitrary"),
        name="in_proj",
    )(x2d, w_main, bf_row, fmask, ones, cw, cb, wg, bg, sp, gn)


def _fox_kernel(qa_ref, ka_ref, v_ref, base_ref, o_ref, vaug_sc, acc_sc, m_sc, *, tq):
    qi = pl.program_id(2)
    seq = v_ref.shape[1]
    lane = lax.broadcasted_iota(jnp.int32, (ROWS, 128), 1)
    sub = lax.broadcasted_iota(jnp.int32, (ROWS, 128), 0)

    @pl.when(qi == 0)
    def _():
        lane_v = lax.broadcasted_iota(jnp.int32, (tq, 128), 1)
        for c in range(seq // tq):
            vc = v_ref[0, c * tq:(c + 1) * tq, :].astype(F32)
            vaug_sc[0, c * tq:(c + 1) * tq, :] = jnp.concatenate(
                [jnp.where(lane_v < B_HEAD_DIM, vc, 0.0), jnp.where(lane_v == 0, 1.0, 0.0)], axis=1).astype(BF16)
            vaug_sc[1, c * tq:(c + 1) * tq, :] = jnp.concatenate(
                [jnp.where(lane_v >= B_HEAD_DIM, vc, 0.0), jnp.where(lane_v == 1, 1.0, 0.0)], axis=1).astype(BF16)

    acc_sc[...] = jnp.zeros_like(acc_sc)
    m_sc[...] = jnp.full_like(m_sc, NEG)
    base_q = base_ref[0, qi, 0]

    def tasks_of(k0, base_k, diag):
        deltas = [base_q[j:j + 1, :] - base_k[j:j + 1, :] for j in range(2)]
        return [(k0, deltas, diag, r) for r in range(tq // ROWS)]

    def qk(task):
        k0, _, diag, r = task
        n = (r + 1) * ROWS if diag else tq
        return [lax.dot_general(qa_ref[0, j, r * ROWS:(r + 1) * ROWS, :], ka_ref[0, j, pl.ds(k0, n), :],
                                (((1,), (1,)), ((), ())), preferred_element_type=F32) for j in range(2)]

    def finish(task, s_pair):
        k0, deltas, diag, r = task
        rows = slice(r * ROWS, (r + 1) * ROWS)
        n = (r + 1) * ROWS if diag else tq
        ps, alphas = [], []
        for j in range(2):
            s = s_pair[j]
            tiles = [s[:, c * 128:(c + 1) * 128] for c in range(n // 128)]
            if diag:
                tiles[-1] = jnp.where(lane <= sub, tiles[-1], NEG)
            tmax = tiles[0]
            for t in tiles[1:]:
                tmax = jnp.maximum(tmax, t)
            m_cur = jnp.max(tmax, axis=-1, keepdims=True) + deltas[j]
            m_prev = m_sc[j, rows, :]
            m_new = jnp.maximum(m_prev, m_cur)
            alphas.append(jnp.exp2(m_prev - m_new))
            m_sc[j, rows, :] = m_new
            mt = m_new - deltas[j]
            ps.append(jnp.concatenate([jnp.exp2(t - mt) for t in tiles], axis=1).astype(BF16))
        vk = jnp.concatenate([vaug_sc[0, pl.ds(k0, n), :], vaug_sc[1, pl.ds(k0, n), :]], axis=0)
        pv = jnp.dot(jnp.concatenate(ps, axis=1), vk, preferred_element_type=F32)
        a = jnp.concatenate([jnp.where(lane < B_HEAD_DIM, alphas[0], alphas[1]),
                             jnp.where(lane == 0, alphas[0], alphas[1])], axis=1)
        acc_sc[rows, :] = a * acc_sc[rows, :] + pv

    def run(tasks):
        s_next = qk(tasks[0])
        for i, task in enumerate(tasks):
            s_cur = s_next
            if i + 1 < len(tasks):
                s_next = qk(tasks[i + 1])
            finish(task, s_cur)

    def past_tasks(ki):
        return tasks_of(pl.multiple_of(ki * tq, tq), base_ref[0, ki, 0], False)

    def body(i, carry):
        run([t for u in range(4) for t in past_tasks(4 * i + u)])
        return carry

    lax.fori_loop(0, qi // 4, body, 0)
    rem0 = (qi // 4) * 4

    for tail in range(4):
        @pl.when(qi % 4 == tail)
        def _(tail=tail):
            run([t for u in range(tail) for t in past_tasks(rem0 + u)]
                + tasks_of(pl.multiple_of(qi * tq, tq), base_q, True))
            acc = acc_sc[...]
            lane_o = lax.broadcasted_iota(jnp.int32, (tq, 128), 1)
            o_ref[0] = acc[:, :128] / jnp.where(lane_o < B_HEAD_DIM, acc[:, 128:129], acc[:, 129:130])


def _fox(qa, ka, v3, base, tq):
    bsz, seq, _ = v3.shape
    ns = seq // tq
    return pl.pallas_call(
        functools.partial(_fox_kernel, tq=tq),
        grid=(bsz, HEAD_PAIRS, ns),
        in_specs=[pl.BlockSpec((1, 2, tq, 128), lambda b, h, i: (b, h, i, 0)),
                  pl.BlockSpec((1, 2, seq, 128), lambda b, h, i: (b, h, 0, 0)),
                  pl.BlockSpec((1, seq, 128), lambda b, h, i: (b, 0, h)),
                  pl.BlockSpec((1, ns, 1, 8, 128), lambda b, h, i: (b, 0, h, 0, 0))],
        out_specs=pl.BlockSpec((1, tq, 128), lambda b, h, i: (b, i, h)),
        out_shape=jax.ShapeDtypeStruct((bsz, seq, D_B), F32),
        scratch_shapes=[pltpu.VMEM((2, seq, 256), BF16), pltpu.VMEM((tq, 256), F32),
                        pltpu.VMEM((2, tq, 128), F32)],
        compiler_params=_cparams("parallel", "parallel", "arbitrary"),
        name="fox_attention",
    )(qa, ka, v3, base)


def _s5_kernel(u_ref, m_ref, bm_ref, cm_ref, lev_ref, pw_ref, d_ref, wglu_ref, gn_ref, o_ref, hs_sc):
    u = u_ref[...]
    rows = u.shape[0]
    groups = rows // 8
    ub = u.astype(BF16)
    sub8 = lax.broadcasted_iota(jnp.int32, (8, S5_HSTATE), 0)
    y_halves = []
    for h in range(2):
        re_cols = slice((2 * h) * S5_HSTATE, (2 * h + 1) * S5_HSTATE)
        im_cols = slice((2 * h + 1) * S5_HSTATE, (2 * h + 2) * S5_HSTATE)
        uh = jnp.concatenate([ub[:, t * D_C + h * 128: t * D_C + (h + 1) * 128] for t in range(S5_CHUNK)],
                             axis=1)
        y = jnp.dot(uh, m_ref[h], preferred_element_type=F32)
        st = jnp.dot(uh, bm_ref[h], preferred_element_type=F32)
        re3 = st[:, :S5_HSTATE].reshape(groups, 8, S5_HSTATE)
        im3 = st[:, S5_HSTATE:].reshape(groups, 8, S5_HSTATE)
        for lv, d in enumerate((1, 2, 4)):
            ar = jnp.where(sub8 >= d, lev_ref[lv:lv + 1, re_cols], 0.0)
            ai = jnp.where(sub8 >= d, lev_ref[lv:lv + 1, im_cols], 0.0)
            rr = pltpu.roll(re3, d, axis=1)
            ri = pltpu.roll(im3, d, axis=1)
            re3, im3 = re3 + (ar * rr - ai * ri), im3 + (ar * ri + ai * rr)
        pr, pi = pw_ref[:, re_cols], pw_ref[:, im_cols]
        cre = jnp.zeros((1, S5_HSTATE), F32)
        cim = jnp.zeros((1, S5_HSTATE), F32)
        for g in range(groups):
            hr = re3[g] + (pr * cre - pi * cim)
            hi = im3[g] + (pr * cim + pi * cre)
            hs_sc[0, 8 * g:8 * g + 8, :] = hr
            hs_sc[1, 8 * g:8 * g + 8, :] = hi
            cre, cim = hr[7:8, :], hi[7:8, :]
        hprev = jnp.concatenate([_shift_rows(hs_sc[0], 1), _shift_rows(hs_sc[1], 1)], axis=1)
        y = y + jnp.dot(hprev.astype(BF16), cm_ref[h], preferred_element_type=F32)
        y_halves.append(y)
    outs = []
    for t in range(S5_CHUNK):
        yt = jnp.concatenate([y_halves[0][:, t * 128:(t + 1) * 128], y_halves[1][:, t * 128:(t + 1) * 128]], axis=1)
        yt = _gelu(yt + d_ref[...] * u[:, t * D_C:(t + 1) * D_C])
        gl = jnp.dot(yt.astype(BF16), wglu_ref[...], preferred_element_type=F32)
        outs.append(_rms_norm(yt * jax.nn.sigmoid(gl), gn_ref[...]))
    o_ref[...] = jnp.concatenate(outs, axis=1)


def _s5(u2, m, bm, cm, lev, pw, d, wglu, gn, rows_per_seq):
    n, width = u2.shape
    blk = pl.BlockSpec((rows_per_seq, width), lambda b: (b, 0))
    return pl.pallas_call(
        _s5_kernel,
        grid=(n // rows_per_seq,),
        in_specs=[blk, _const_spec(m.shape), _const_spec(bm.shape), _const_spec(cm.shape),
                  _const_spec(lev.shape), _const_spec(pw.shape), _const_spec((1, D_C)), _const_spec((D_C, D_C)),
                  _const_spec((1, D_C))],
        out_specs=blk,
        out_shape=jax.ShapeDtypeStruct(u2.shape, F32),
        scratch_shapes=[pltpu.VMEM((2, rows_per_seq, S5_HSTATE), F32)],
        compiler_params=_cparams("parallel"),
        name="s5",
    )(u2, m, bm, cm, lev, pw, d, wglu, gn)


def _s5_prepare(a_re, a_im, log_dt, b_re, b_im, c_re, c_im):
    hi = lax.Precision.HIGHEST
    T = S5_CHUNK
    dt = jnp.exp(log_dt)[:, None]
    lr, li = a_re * dt, a_im * dt

    def cpow(tau):
        mag = jnp.exp(lr[None] * tau[:, None, None])
        ang = li[None] * tau[:, None, None]
        return mag * jnp.cos(ang), mag * jnp.sin(ang)

    ar, ai = cpow(jnp.arange(T + 1, dtype=F32))
    nr, ni = ar[1] - 1.0, ai[1]
    den = a_re * a_re + a_im * a_im
    fr, fi = (nr * a_re + ni * a_im) / den, (ni * a_re - nr * a_im) / den
    bbr = fr[:, :, None] * b_re - fi[:, :, None] * b_im
    bbi = fr[:, :, None] * b_im + fi[:, :, None] * b_re
    abr = ar[:, :, :, None] * bbr[None] - ai[:, :, :, None] * bbi[None]
    abi = ar[:, :, :, None] * bbi[None] + ai[:, :, :, None] * bbr[None]
    kern = (jnp.einsum('gop,tgpi->tgio', c_re, abr[:T], precision=hi)
            - jnp.einsum('gop,tgpi->tgio', c_im, abi[:T], precision=hi))
    tin = jnp.arange(T)[:, None]
    tout = jnp.arange(T)[None, :]
    lag = tout - tin
    ktt = jnp.where((lag >= 0)[:, :, None, None, None], kern[jnp.clip(lag, 0, T - 1)], 0.0)
    eye = jnp.eye(S5_HALF, dtype=F32)
    rev = T - 1 - jnp.arange(T)
    sin_r, sin_i = abr[rev], abi[rev]
    so_r = c_re[None] * ar[1:, :, None, :] - c_im[None] * ai[1:, :, None, :]
    so_i = c_re[None] * ai[1:, :, None, :] + c_im[None] * ar[1:, :, None, :]
    ms, bms, cms = [], [], []
    for h in range(2):
        gs = slice(h * S5_HALF, (h + 1) * S5_HALF)
        mh = jnp.einsum('abgio,gk->agibko', ktt[:, :, gs], eye)
        ms.append(mh.reshape(S5_HCOLS, S5_HCOLS))
        parts = [jnp.einsum('agpi,gk->agikp', s[:, gs], eye).reshape(S5_HCOLS, S5_HSTATE) for s in (sin_r, sin_i)]
        bms.append(jnp.concatenate(parts, axis=1))
        parts = [jnp.einsum('bgop,gk->gpbko', s[:, gs], eye).reshape(S5_HSTATE, S5_HCOLS) for s in (so_r, -so_i)]
        cms.append(jnp.concatenate(parts, axis=0))
    def state_layout(zr, zi):
        n = zr.shape[0]
        parts = []
        for h in range(2):
            gs = slice(h * S5_HALF, (h + 1) * S5_HALF)
            parts += [zr[:, gs].reshape(n, S5_HSTATE), zi[:, gs].reshape(n, S5_HSTATE)]
        return jnp.concatenate(parts, axis=1)

    lev = state_layout(*cpow(jnp.asarray([T, 2 * T, 4 * T], F32)))
    pw = state_layout(*cpow(T * jnp.arange(1, 9, dtype=F32)))
    return (jnp.stack(ms).astype(BF16), jnp.stack(bms).astype(BF16), jnp.stack(cms).astype(BF16), lev, pw)


def _mix_ffn_kernel(oa_ref, ob_ref, oc_ref, x_ref, gb_ref, wo_ref, g2_ref, b2_ref, wg_ref, wu_ref, wd_ref,
                    g3_ref, b3_ref, o_ref, oc_sc):
    tm = x_ref.shape[0]
    for t in range(S5_CHUNK):
        for j in range(2):
            oc_sc[j, pl.ds(t, tm // S5_CHUNK, stride=S5_CHUNK), :] = oc_ref[:, t * D_C + j * 128:t * D_C + (j + 1) * 128]
    nb = _rms_norm(ob_ref[...], gb_ref[...]).astype(BF16)
    o = jnp.concatenate([oa_ref[...], nb, oc_sc[0].astype(BF16), oc_sc[1].astype(BF16)], axis=1)
    mix = jnp.dot(o, wo_ref[...], preferred_element_type=F32)
    x = _layer_norm(ALPHA * x_ref[...] + mix, g2_ref[...], b2_ref[...])
    o_ref[...] = _ffn_ln_tile(x, wg_ref, wu_ref, wd_ref, g3_ref, b3_ref)


def _mix_ffn(oa2, ob2, oc2, x2d, gb, wo, g2, b2, wg, wu, wd, g3, b3, tm):
    tok = lambda rows, width: pl.BlockSpec((rows, width), lambda i: (i, 0))
    vec = _const_spec((1, D_MODEL))
    return pl.pallas_call(
        _mix_ffn_kernel,
        grid=(x2d.shape[0] // tm,),
        in_specs=[tok(tm, D_A), tok(tm, D_B), tok(tm // S5_CHUNK, S5_CHUNK * D_C), tok(tm, D_MODEL),
                  _const_spec((1, D_B)), _const_spec((D_MODEL, D_MODEL)), vec, vec,
                  _const_spec((D_MODEL, D_FF)), _const_spec((D_MODEL, D_FF)), _const_spec((D_FF, D_MODEL)), vec, vec],
        out_specs=tok(tm, D_MODEL),
        out_shape=jax.ShapeDtypeStruct(x2d.shape, F32),
        scratch_shapes=[pltpu.VMEM((2, tm, 128), F32)],
        compiler_params=_cparams("parallel"),
        name="mix_ffn_ln",
    )(oa2, ob2, oc2, x2d, gb, wo, g2, b2, wg, wu, wd, g3, b3)


def _block_diag(w):
    eye = jnp.eye(A_BLOCKS, dtype=w.dtype)
    return jnp.einsum('hij,hk->hikj', w, eye).reshape(D_A, D_A)


def _pick(n, pref):
    t = min(n, pref)
    while n % t:
        t //= 2
    return t


def kernel(x, ffn1_w_gate, ffn1_w_up, ffn1_w_down, ln1_g, ln1_b, w_in, conv_w, conv_b, rg_w_a, rg_b_a, rg_w_x, rg_b_x, rg_lambda, fox_b_f, s5_a_re, s5_a_im, s5_log_dt, s5_b_re, s5_b_im, s5_c_re, s5_c_im, s5_d, s5_w_glu, mix_norm_g, w_out, ln2_g, ln2_b, ffn2_w_gate, ffn2_w_up, ffn2_w_down, ln3_g, ln3_b):
    bsz, seq, _ = x.shape
    n = bsz * seq
    tm = _pick(seq, 512)
    row = lambda v: v.reshape(1, -1)
    h = x.reshape(n, D_MODEL)
    cuts = np.cumsum([D_A, D_A, D_B, D_B, D_B, B_HEADS])
    for l in range(DEPTH):
        h = _ffn_ln(h, ffn1_w_gate[l].astype(BF16), ffn1_w_up[l].astype(BF16), ffn1_w_down[l].astype(BF16),
                    row(ln1_g[l]), row(ln1_b[l]), tm)

        wl = w_in[l]
        fcols = jnp.zeros((D_MODEL, 128), F32)
        bf_row = jnp.zeros((128,), F32)
        fmask = np.zeros((1, 128), np.float32)
        ones = np.zeros((B_HEADS, 128), np.float32)
        for piece in range(3):
            lo_ = F_LANE0 + 8 * piece
            fcols = fcols.at[:, lo_:lo_ + B_HEADS].set(wl[:, cuts[4]:cuts[5]])
            bf_row = bf_row.at[lo_:lo_ + B_HEADS].set(fox_b_f[l])
            fmask[0, lo_:lo_ + B_HEADS] = 1.0
            ones[np.arange(B_HEADS), lo_ + np.arange(B_HEADS)] = 1.0
        w_main = jnp.concatenate([wl[:, :cuts[4]], wl[:, cuts[5]:], fcols], axis=1).astype(BF16)
        wgates = jnp.concatenate([_block_diag(rg_w_a[l]), _block_diag(rg_w_x[l])], axis=1).astype(BF16)
        bgates = jnp.concatenate([rg_b_a[l], rg_b_x[l]]).reshape(1, 2 * D_A)
        g_a, g_b, g_c = jnp.split(mix_norm_g[l], [D_A, D_A + D_B])
        oa, qa, ka, v, cu, base = _in_proj(h, w_main, bf_row.reshape(1, 128), jnp.asarray(fmask), jnp.asarray(ones),
                                           conv_w[l], row(conv_b[l]), wgates, bgates,
                                           row(jax.nn.softplus(-rg_lambda[l])), row(g_a), bsz, seq, tm)

        ob = _fox(qa, ka, v.reshape(bsz, seq, D_B), base, tm)

        m, bm, cm, lev, pw = _s5_prepare(s5_a_re[l], s5_a_im[l], s5_log_dt[l], s5_b_re[l], s5_b_im[l],
                                         s5_c_re[l], s5_c_im[l])
        oc = _s5(cu, m, bm, cm, lev, pw, row(s5_d[l]), s5_w_glu[l].astype(BF16), row(g_c), seq // S5_CHUNK)

        h = _mix_ffn(oa, ob.reshape(n, D_B), oc, h, row(g_b), w_out[l].astype(BF16), row(ln2_g[l]), row(ln2_b[l]),
                     ffn2_w_gate[l].astype(BF16), ffn2_w_up[l].astype(BF16), ffn2_w_down[l].astype(BF16),
                     row(ln3_g[l]), row(ln3_b[l]), tm)
    return h.reshape(bsz, seq, D_MODEL)
```

```python
import functools
import math

import numpy as np
import jax
import jax.numpy as jnp
from jax import lax
from jax.experimental import pallas as pl
from jax.experimental.pallas import tpu as pltpu

F32 = jnp.float32
BF16 = jnp.bfloat16

D_MODEL = 1024
D_FF = 2816
D_A = 384
A_BLOCKS = 6
A_BLOCK_DIM = 64
CONV_WIDTH = 4
RG_C = 8.0
D_B = 384
B_HEADS = 6
B_HEAD_DIM = 64
HEAD_PAIRS = B_HEADS // 2
D_C = 256
C_GROUP = 16
C_GROUPS = 16
C_STATE = 64
W_SPLIT = 1792
W_TAIL = 2 * D_A + 3 * D_B - W_SPLIT + D_C + 128
F_LANE0 = 64
LOG2E = math.log2(math.e)
QSCALE = LOG2E * B_HEAD_DIM ** -0.5
NEG = -1e30
ROWS = 128
FFN_ROWS = 256
DEPTH = 2
ALPHA = (2 * DEPTH) ** 0.25
LN_EPS = 1e-5
RMS_EPS = 1e-6

S5_CHUNK = 8
S5_HALF = C_GROUPS // 2
S5_HCOLS = S5_CHUNK * S5_HALF * C_GROUP
S5_HSTATE = S5_HALF * C_STATE

VMEM_LIMIT = 56 * 1024 * 1024


def _cparams(*sem):
    return pltpu.CompilerParams(dimension_semantics=sem, vmem_limit_bytes=VMEM_LIMIT)


def _const_spec(shape):
    nd = len(shape)
    return pl.BlockSpec(shape, lambda *_: (0,) * nd, pipeline_mode=pl.Buffered(1))


def _layer_spec(arr, l, block=None):
    block = tuple(arr.shape[1:]) if block is None else block
    nd = len(block)
    return pl.BlockSpec((None,) + block, lambda *_: (l,) + (0,) * nd, pipeline_mode=pl.Buffered(1))


def _layer_norm(r, g, b):
    mu = jnp.mean(r, axis=-1, keepdims=True)
    d = r - mu
    var = jnp.mean(d * d, axis=-1, keepdims=True)
    return d * lax.rsqrt(var + LN_EPS) * g + b


def _rms_norm(x, g):
    return x * lax.rsqrt(jnp.mean(x * x, axis=-1, keepdims=True) + RMS_EPS) * g


def _gelu(x):
    return 0.5 * x * (1.0 + jnp.tanh(math.sqrt(2.0 / math.pi) * (x + 0.044715 * (x * x * x))))


def _log_sigmoid(x):
    return -(jnp.maximum(-x, 0.0) + jnp.log1p(jnp.exp(-jnp.abs(x))))


def _shift_rows(x, d):
    rows, cols = x.shape
    groups = rows // 8
    x3 = x.reshape(groups, 8, cols)
    if d % 8 == 0:
        return jnp.concatenate([jnp.zeros((d // 8, 8, cols), x.dtype), x3[:groups - d // 8]], axis=0).reshape(rows, cols)
    sub = lax.broadcasted_iota(jnp.int32, (groups, 8, cols), 1)
    rolled = pltpu.roll(x3, d, axis=1)
    before = jnp.concatenate([jnp.zeros((1, 8, cols), x.dtype), rolled[:groups - 1]], axis=0)
    return jnp.where(sub >= d, rolled, before).reshape(rows, cols)


def _ffn_ln_tile(x, wg_ref, wu_ref, wd_ref, g_ref, b_ref):
    xb = x.astype(BF16)
    gate = jnp.dot(xb, wg_ref[...], preferred_element_type=F32)
    up = jnp.dot(xb, wu_ref[...], preferred_element_type=F32)
    h = (gate * jax.nn.sigmoid(gate) * up).astype(BF16)
    y = jnp.dot(h, wd_ref[...], preferred_element_type=F32)
    return _layer_norm(ALPHA * x + 0.5 * y, g_ref[...], b_ref[...])


def _ffn_ln_kernel(x_ref, wg_ref, wu_ref, wd_ref, g_ref, b_ref, o_ref):
    for c in range(x_ref.shape[0] // FFN_ROWS):
        rows = slice(c * FFN_ROWS, (c + 1) * FFN_ROWS)
        o_ref[rows, :] = _ffn_ln_tile(x_ref[rows, :], wg_ref, wu_ref, wd_ref, g_ref, b_ref)


def _ffn_ln(x2d, wg, wu, wd, g, b, l, tm):
    n = x2d.shape[0]
    row = pl.BlockSpec((tm, D_MODEL), lambda i: (i, 0))
    return pl.pallas_call(
        _ffn_ln_kernel,
        grid=(n // tm,),
        in_specs=[row] + [_layer_spec(a, l) for a in (wg, wu, wd, g, b)],
        out_specs=row,
        out_shape=jax.ShapeDtypeStruct((n, D_MODEL), F32),
        compiler_params=_cparams("parallel"),
        name="ffn_ln",
    )(x2d, wg, wu, wd, g, b)


def _rglru_tile(ax, ag, cw_ref, cb_ref, wg_ref, bg_ref, sp_ref, gn_ref, prev8_sc, hcarry_sc, hs_sc):
    tm = ax.shape[0]
    groups = tm // 8
    grouped = lambda v: v.reshape(groups, 8, D_A)
    sub = lax.broadcasted_iota(jnp.int32, (groups, 8, D_A), 1)
    ax3 = grouped(ax)
    prev8 = prev8_sc[...]
    xa = cb_ref[...] + cw_ref[CONV_WIDTH - 1:CONV_WIDTH, :] * ax3
    for k in range(1, CONV_WIDTH):
        rolled = pltpu.roll(ax3, k, axis=1)
        before = jnp.concatenate([pltpu.roll(prev8, k, axis=0)[None], rolled[:groups - 1]], axis=0)
        xa = xa + cw_ref[CONV_WIDTH - 1 - k:CONV_WIDTH - k, :] * jnp.where(sub >= k, rolled, before)
    prev8_sc[...] = ax3[groups - 1]
    xa = xa.reshape(tm, D_A)

    gates = jnp.dot(xa.astype(BF16), wg_ref[...], preferred_element_type=F32) + bg_ref[...]
    r = jax.nn.sigmoid(gates[:, :D_A])
    i = jax.nn.sigmoid(gates[:, D_A:])
    log_a = (-RG_C) * r * sp_ref[...]
    a = jnp.exp(log_a)
    y = -jnp.tanh(log_a) * (1.0 + a * a)
    b = jnp.where(y > 0.0, y * lax.rsqrt(y), 0.0) * (i * xa)

    a3, b3 = grouped(a), grouped(b)
    for d in (1, 2, 4):
        valid = sub >= d
        b3 = jnp.where(valid, a3 * pltpu.roll(b3, d, axis=1) + b3, b3)
        a3 = jnp.where(valid, a3 * pltpu.roll(a3, d, axis=1), a3)
    last = hcarry_sc[...]
    for g in range(groups):
        hg = b3[g] + a3[g] * last
        hs_sc[8 * g:8 * g + 8, :] = hg
        last = hg[7:8, :]
    hcarry_sc[...] = last
    return _rms_norm(_gelu(ag) * hs_sc[...], gn_ref[...])


def _in_proj_kernel(x_ref, wa_ref, wb_ref, bf_ref, fmask_ref, ones_ref, cw_ref, cb_ref, wg_ref, bg_ref, sp_ref,
                    gn_ref, oa_ref, qa_ref, ka_ref, v_ref, cu_ref, base_ref,
                    carry_sc, prev8_sc, hcarry_sc, hs_sc, cu_sc):
    tm = x_ref.shape[0]

    @pl.when(pl.program_id(1) == 0)
    def _():
        carry_sc[...] = jnp.zeros_like(carry_sc)
        prev8_sc[...] = jnp.zeros_like(prev8_sc)
        hcarry_sc[...] = jnp.zeros_like(hcarry_sc)

    xb = x_ref[...].astype(BF16)
    z = jnp.dot(xb, wa_ref[...], preferred_element_type=F32)
    zb = jnp.dot(xb, wb_ref[...], preferred_element_type=F32)
    o = 2 * D_A
    v_rest = W_SPLIT - (o + 2 * D_B)
    v_ref[...] = jnp.concatenate([z[:, o + 2 * D_B:], zb[:, :D_B - v_rest]], axis=1).astype(BF16)

    oa_ref[...] = _rglru_tile(z[:, 0:D_A], z[:, D_A:2 * D_A], cw_ref, cb_ref, wg_ref, bg_ref, sp_ref, gn_ref,
                              prev8_sc, hcarry_sc, hs_sc).astype(BF16)

    cu = zb[:, D_B - v_rest:D_B - v_rest + D_C]
    cu_sc[0] = cu[:, :128]
    cu_sc[1] = cu[:, 128:]
    cu_ref[...] = jnp.concatenate([cu_sc[j, pl.ds(t, tm // S5_CHUNK, stride=S5_CHUNK), :]
                                   for t in range(S5_CHUNK) for j in range(2)], axis=1)

    lf = _log_sigmoid(zb[:, W_TAIL - 128:] + bf_ref[...]) * fmask_ref[...] * LOG2E
    lane = lax.broadcasted_iota(jnp.int32, (tm, 128), 1)
    c = lf
    d = 1
    while d < tm:
        c = c + _shift_rows(c, d)
        d *= 2
    carry = carry_sc[...]
    rows = [jnp.broadcast_to(carry[:, F_LANE0 + h:F_LANE0 + h + 1], (1, 128)) for h in range(B_HEADS)]
    pad = jnp.zeros((6, 128), F32)
    base_ref[0, 0] = jnp.stack([jnp.concatenate([rows[2 * p], rows[2 * p + 1], pad], axis=0)
                                for p in range(HEAD_PAIRS)])
    carry_sc[...] = carry + c[tm - 1:tm, :]

    hi = c.astype(BF16).astype(F32)
    r1 = c - hi
    mid = r1.astype(BF16).astype(F32)
    lo = (r1 - mid).astype(BF16).astype(F32)
    pieces = jnp.where(lane < F_LANE0 + 8, hi, jnp.where(lane < F_LANE0 + 16, mid, lo))
    for p in range(HEAD_PAIRS):
        qt = z[:, o + 128 * p:o + 128 * (p + 1)]
        kt = z[:, o + D_B + 128 * p:o + D_B + 128 * (p + 1)]
        for e in range(2):
            h = 2 * p + e
            qh = pltpu.roll(qt, B_HEAD_DIM, axis=1) if e else qt
            kh = pltpu.roll(kt, B_HEAD_DIM, axis=1) if e else kt
            qa_ref[0, h] = jnp.where(lane < B_HEAD_DIM, qh * QSCALE, ones_ref[h:h + 1, :]).astype(BF16)
            ka_ref[0, h] = jnp.where(lane < B_HEAD_DIM, kh, -pieces).astype(BF16)


def _in_proj(x2d, w16, wb, bf_row, fmask, ones, cw, cb, wg, bg, sp, gn, l, bsz, seq, tm):
    ns = seq // tm
    n = bsz * seq
    tok = lambda rows, width: pl.BlockSpec((rows, width), lambda b, s: (b * ns + s, 0))
    heads = pl.BlockSpec((1, B_HEADS, tm, 128), lambda b, s: (b, 0, s, 0))
    return pl.pallas_call(
        _in_proj_kernel,
        grid=(bsz, ns),
        in_specs=[tok(tm, D_MODEL), _layer_spec(w16, l, (D_MODEL, W_SPLIT)), _layer_spec(wb, l),
                  _layer_spec(bf_row, l), _const_spec((1, 128)), _const_spec((B_HEADS, 128))]
                 + [_layer_spec(a, l) for a in (cw, cb, wg, bg, sp, gn)],
        out_specs=[tok(tm, D_A), heads, heads, tok(tm, D_B), tok(tm // S5_CHUNK, S5_CHUNK * D_C),
                   pl.BlockSpec((1, 1, HEAD_PAIRS, 8, 128), lambda b, s: (b, s, 0, 0, 0))],
        out_shape=[jax.ShapeDtypeStruct((n, D_A), BF16),
                   jax.ShapeDtypeStruct((bsz, B_HEADS, seq, 128), BF16),
                   jax.ShapeDtypeStruct((bsz, B_HEADS, seq, 128), BF16),
                   jax.ShapeDtypeStruct((n, D_B), BF16),
                   jax.ShapeDtypeStruct((n // S5_CHUNK, S5_CHUNK * D_C), F32),
                   jax.ShapeDtypeStruct((bsz, ns, HEAD_PAIRS, 8, 128), F32)],
        scratch_shapes=[pltpu.VMEM((1, 128), F32), pltpu.VMEM((8, D_A), F32), pltpu.VMEM((1, D_A), F32),
                        pltpu.VMEM((tm, D_A), F32), pltpu.VMEM((2, tm, 128), F32)],
        compiler_params=_cparams("parallel", "arbitrary"),
        name="in_proj",
    )(x2d, w16, wb, bf_row, fmask, ones, cw, cb, wg, bg, sp, gn)


def _fox_kernel(qa_ref, ka_ref, v_ref, base_ref, o_ref, vaug_sc, acc_sc, m_sc, *, tq):
    qi = pl.program_id(2)
    seq = v_ref.shape[1]
    lane = lax.broadcasted_iota(jnp.int32, (ROWS, 128), 1)
    sub = lax.broadcasted_iota(jnp.int32, (ROWS, 128), 0)

    @pl.when(qi == 0)
    def _():
        lane_v = lax.broadcasted_iota(jnp.int32, (tq, 128), 1)
        for c in range(seq // tq):
            vc = v_ref[0, c * tq:(c + 1) * tq, :].astype(F32)
            vaug_sc[0, c * tq:(c + 1) * tq, :] = jnp.concatenate(
                [jnp.where(lane_v < B_HEAD_DIM, vc, 0.0), jnp.where(lane_v == 0, 1.0, 0.0)], axis=1).astype(BF16)
            vaug_sc[1, c * tq:(c + 1) * tq, :] = jnp.concatenate(
                [jnp.where(lane_v >= B_HEAD_DIM, vc, 0.0), jnp.where(lane_v == 1, 1.0, 0.0)], axis=1).astype(BF16)

    acc_sc[...] = jnp.zeros_like(acc_sc)
    m_sc[...] = jnp.full_like(m_sc, NEG)
    base_q = base_ref[0, qi, 0]

    def tasks_of(k0, base_k, diag):
        deltas = [base_q[j:j + 1, :] - base_k[j:j + 1, :] for j in range(2)]
        return [(k0, deltas, diag, r) for r in range(tq // ROWS)]

    def qk(task):
        k0, _, diag, r = task
        n = (r + 1) * ROWS if diag else tq
        return [lax.dot_general(qa_ref[0, j, r * ROWS:(r + 1) * ROWS, :], ka_ref[0, j, pl.ds(k0, n), :],
                                (((1,), (1,)), ((), ())), preferred_element_type=F32) for j in range(2)]

    def finish(task, s_pair):
        k0, deltas, diag, r = task
        rows = slice(r * ROWS, (r + 1) * ROWS)
        n = (r + 1) * ROWS if diag else tq
        ps, alphas = [], []
        for j in range(2):
            s = s_pair[j]
            tiles = [s[:, c * 128:(c + 1) * 128] for c in range(n // 128)]
            if diag:
                tiles[-1] = jnp.where(lane <= sub, tiles[-1], NEG)
            tmax = tiles[0]
            for t in tiles[1:]:
                tmax = jnp.maximum(tmax, t)
            m_cur = jnp.max(tmax, axis=-1, keepdims=True) + deltas[j]
            m_prev = m_sc[j, rows, :]
            m_new = jnp.maximum(m_prev, m_cur)
            alphas.append(jnp.exp2(m_prev - m_new))
            m_sc[j, rows, :] = m_new
            mt = m_new - deltas[j]
            ps.append(jnp.concatenate([jnp.exp2(t - mt) for t in tiles], axis=1).astype(BF16))
        vk = jnp.concatenate([vaug_sc[0, pl.ds(k0, n), :], vaug_sc[1, pl.ds(k0, n), :]], axis=0)
        pv = jnp.dot(jnp.concatenate(ps, axis=1), vk, preferred_element_type=F32)
        a = jnp.concatenate([jnp.where(lane < B_HEAD_DIM, alphas[0], alphas[1]),
                             jnp.where(lane == 0, alphas[0], alphas[1])], axis=1)
        acc_sc[rows, :] = a * acc_sc[rows, :] + pv

    def run(tasks):
        s_next = qk(tasks[0])
        for i, task in enumerate(tasks):
            s_cur = s_next
            if i + 1 < len(tasks):
                s_next = qk(tasks[i + 1])
            finish(task, s_cur)

    def past_tasks(ki):
        return tasks_of(pl.multiple_of(ki * tq, tq), base_ref[0, ki, 0], False)

    def body(i, carry):
        run([t for u in range(4) for t in past_tasks(4 * i + u)])
        return carry

    lax.fori_loop(0, qi // 4, body, 0)
    rem0 = (qi // 4) * 4

    for tail in range(4):
        @pl.when(qi % 4 == tail)
        def _(tail=tail):
            run([t for u in range(tail) for t in past_tasks(rem0 + u)]
                + tasks_of(pl.multiple_of(qi * tq, tq), base_q, True))
            acc = acc_sc[...]
            lane_o = lax.broadcasted_iota(jnp.int32, (tq, 128), 1)
            o_ref[0] = acc[:, :128] / jnp.where(lane_o < B_HEAD_DIM, acc[:, 128:129], acc[:, 129:130])


def _fox(qa, ka, v3, base, tq):
    bsz, seq, _ = v3.shape
    ns = seq // tq
    return pl.pallas_call(
        functools.partial(_fox_kernel, tq=tq),
        grid=(bsz, HEAD_PAIRS, ns),
        in_specs=[pl.BlockSpec((1, 2, tq, 128), lambda b, h, i: (b, h, i, 0)),
                  pl.BlockSpec((1, 2, seq, 128), lambda b, h, i: (b, h, 0, 0)),
                  pl.BlockSpec((1, seq, 128), lambda b, h, i: (b, 0, h)),
                  pl.BlockSpec((1, ns, 1, 8, 128), lambda b, h, i: (b, 0, h, 0, 0))],
        out_specs=pl.BlockSpec((1, tq, 128), lambda b, h, i: (b, i, h)),
        out_shape=jax.ShapeDtypeStruct((bsz, seq, D_B), F32),
        scratch_shapes=[pltpu.VMEM((2, seq, 256), BF16), pltpu.VMEM((tq, 256), F32),
                        pltpu.VMEM((2, tq, 128), F32)],
        compiler_params=_cparams("parallel", "parallel", "arbitrary"),
        name="fox_attention",
    )(qa, ka, v3, base)


def _s5_kernel(u_ref, m_ref, bm_ref, cm_ref, lev_ref, pw_ref, d_ref, wglu_ref, gn_ref, o_ref, hs_sc):
    u = u_ref[...]
    rows = u.shape[0]
    groups = rows // 8
    ub = u.astype(BF16)
    sub8 = lax.broadcasted_iota(jnp.int32, (8, S5_HSTATE), 0)
    y_halves = []
    for h in range(2):
        re_cols = slice((2 * h) * S5_HSTATE, (2 * h + 1) * S5_HSTATE)
        im_cols = slice((2 * h + 1) * S5_HSTATE, (2 * h + 2) * S5_HSTATE)
        uh = jnp.concatenate([ub[:, t * D_C + h * 128: t * D_C + (h + 1) * 128] for t in range(S5_CHUNK)],
                             axis=1)
        y = jnp.dot(uh, m_ref[h], preferred_element_type=F32)
        st = jnp.dot(uh, bm_ref[h], preferred_element_type=F32)
        re3 = st[:, :S5_HSTATE].reshape(groups, 8, S5_HSTATE)
        im3 = st[:, S5_HSTATE:].reshape(groups, 8, S5_HSTATE)
        for lv, d in enumerate((1, 2, 4)):
            ar = jnp.where(sub8 >= d, lev_ref[lv:lv + 1, re_cols], 0.0)
            ai = jnp.where(sub8 >= d, lev_ref[lv:lv + 1, im_cols], 0.0)
            rr = pltpu.roll(re3, d, axis=1)
            ri = pltpu.roll(im3, d, axis=1)
            re3, im3 = re3 + (ar * rr - ai * ri), im3 + (ar * ri + ai * rr)
        pr, pi = pw_ref[:, re_cols], pw_ref[:, im_cols]
        cre = jnp.zeros((1, S5_HSTATE), F32)
        cim = jnp.zeros((1, S5_HSTATE), F32)
        for g in range(groups):
            hr = re3[g] + (pr * cre - pi * cim)
            hi = im3[g] + (pr * cim + pi * cre)
            hs_sc[0, 8 * g:8 * g + 8, :] = hr
            hs_sc[1, 8 * g:8 * g + 8, :] = hi
            cre, cim = hr[7:8, :], hi[7:8, :]
        hprev = jnp.concatenate([_shift_rows(hs_sc[0], 1), _shift_rows(hs_sc[1], 1)], axis=1)
        y = y + jnp.dot(hprev.astype(BF16), cm_ref[h], preferred_element_type=F32)
        y_halves.append(y)
    outs = []
    for t in range(S5_CHUNK):
        yt = jnp.concatenate([y_halves[0][:, t * 128:(t + 1) * 128], y_halves[1][:, t * 128:(t + 1) * 128]], axis=1)
        yt = _gelu(yt + d_ref[...] * u[:, t * D_C:(t + 1) * D_C])
        gl = jnp.dot(yt.astype(BF16), wglu_ref[...], preferred_element_type=F32)
        outs.append(_rms_norm(yt * jax.nn.sigmoid(gl), gn_ref[...]))
    o_ref[...] = jnp.concatenate(outs, axis=1)


def _s5(u2, m, bm, cm, lev, pw, d, wglu, gn, l, rows_per_seq):
    n, width = u2.shape
    blk = pl.BlockSpec((rows_per_seq, width), lambda b: (b, 0))
    return pl.pallas_call(
        _s5_kernel,
        grid=(n // rows_per_seq,),
        in_specs=[blk] + [_layer_spec(a, l) for a in (m, bm, cm, lev, pw, d, wglu, gn)],
        out_specs=blk,
        out_shape=jax.ShapeDtypeStruct(u2.shape, F32),
        scratch_shapes=[pltpu.VMEM((2, rows_per_seq, S5_HSTATE), F32)],
        compiler_params=_cparams("parallel"),
        name="s5",
    )(u2, m, bm, cm, lev, pw, d, wglu, gn)


def _s5_prepare(a_re, a_im, log_dt, b_re, b_im, c_re, c_im):
    hi = lax.Precision.HIGHEST
    T = S5_CHUNK
    nl = a_re.shape[0]
    dt = jnp.exp(log_dt)[:, :, None]
    lr, li = a_re * dt, a_im * dt

    def cpow(tau):
        mag = jnp.exp(lr[:, None] * tau[None, :, None, None])
        ang = li[:, None] * tau[None, :, None, None]
        return mag * jnp.cos(ang), mag * jnp.sin(ang)

    def halves(z):
        return jnp.moveaxis(z.reshape(z.shape[:2] + (2, S5_HALF) + z.shape[3:]), 2, 1)

    ar1, ai1 = cpow(jnp.ones((1,), F32))
    nr, ni = ar1[:, 0] - 1.0, ai1[:, 0]
    den = a_re * a_re + a_im * a_im
    fr, fi = (nr * a_re + ni * a_im) / den, (ni * a_re - nr * a_im) / den
    bbr = fr[..., None] * b_re - fi[..., None] * b_im
    bbi = fr[..., None] * b_im + fi[..., None] * b_re

    dr, di = cpow((T - 1) - jnp.arange(T, dtype=F32))
    abr = dr[..., None] * bbr[:, None] - di[..., None] * bbi[:, None]
    abi = dr[..., None] * bbi[:, None] + di[..., None] * bbr[:, None]

    def same_group(rows, rdiv, cols, cdiv, cmod=S5_HALF):
        r = (jnp.arange(rows) // rdiv)[:, None]
        c = ((jnp.arange(cols) // cdiv) % cmod)[None, :]
        return (r == c).astype(F32)

    def tile_eye(n, reps):
        return (jnp.arange(n)[:, None] == (jnp.arange(n * reps) % n)[None, :]).astype(F32)

    kern = (jnp.einsum('lgop,ljgpi->ljgio', c_re, abr, precision=hi)
            - jnp.einsum('lgop,ljgpi->ljgio', c_im, abi, precision=hi))
    kd = jnp.einsum('lhjrc,cn->lhjrn', halves(kern).reshape(nl, 2, T, 128, C_GROUP), tile_eye(C_GROUP, S5_HALF))
    kd = kd * same_group(128, C_GROUP, 128, C_GROUP)
    ext = jnp.concatenate([kd, jnp.zeros((nl, 2, T - 1, 128, 128), F32)], axis=2)
    m = jnp.concatenate([ext[:, :, T - 1 - b:2 * T - 1 - b] for b in range(T)], axis=-1)
    m = m.reshape(nl, 2, S5_HCOLS, S5_HCOLS)

    def state_in(ab):
        y = jnp.swapaxes(halves(ab), -1, -2).reshape(nl, 2, T, 128, C_STATE)
        return jnp.einsum('lhjrp,pn->lhjrn', y, tile_eye(C_STATE, S5_HALF)) * same_group(128, C_GROUP, S5_HSTATE, C_STATE)

    bm = jnp.concatenate([state_in(abr), state_in(abi)], axis=-1).reshape(nl, 2, S5_HCOLS, 2 * S5_HSTATE)

    ur, ui = cpow(jnp.arange(1, T + 1, dtype=F32))
    so_r = c_re[:, None] * ur[:, :, :, None, :] - c_im[:, None] * ui[:, :, :, None, :]
    so_i = c_re[:, None] * ui[:, :, :, None, :] + c_im[:, None] * ur[:, :, :, None, :]
    expand = (jnp.arange(T * C_GROUP)[:, None] // C_GROUP == jnp.arange(S5_HCOLS)[None, :] // 128) & (
        jnp.arange(T * C_GROUP)[:, None] % C_GROUP == jnp.arange(S5_HCOLS)[None, :] % C_GROUP)

    def state_out(so):
        y = jnp.transpose(halves(so), (0, 1, 3, 5, 2, 4)).reshape(nl, 2, S5_HSTATE, T * C_GROUP)
        return jnp.einsum('lhrc,cn->lhrn', y, expand.astype(F32)) * same_group(S5_HSTATE, C_STATE, S5_HCOLS, C_GROUP)

    cm = jnp.concatenate([state_out(so_r), state_out(-so_i)], axis=2)

    def state_layout(zr, zi):
        n = zr.shape[1]
        parts = []
        for h in range(2):
            gs = slice(h * S5_HALF, (h + 1) * S5_HALF)
            parts += [zr[:, :, gs].reshape(nl, n, S5_HSTATE), zi[:, :, gs].reshape(nl, n, S5_HSTATE)]
        return jnp.concatenate(parts, axis=-1)

    lev = state_layout(*cpow(jnp.asarray([T, 2 * T, 4 * T], F32)))
    pw = state_layout(*cpow(T * jnp.arange(1, 9, dtype=F32)))
    return m.astype(BF16), bm.astype(BF16), cm.astype(BF16), lev, pw


def _mix_ffn_kernel(oa_ref, ob_ref, oc_ref, x_ref, gb_ref, wo_ref, g2_ref, b2_ref, wg_ref, wu_ref, wd_ref,
                    g3_ref, b3_ref, o_ref, oc_sc):
    tm = x_ref.shape[0]
    for t in range(S5_CHUNK):
        for j in range(2):
            oc_sc[j, pl.ds(t, tm // S5_CHUNK, stride=S5_CHUNK), :] = oc_ref[:, t * D_C + j * 128:t * D_C + (j + 1) * 128]
    nb = _rms_norm(ob_ref[...], gb_ref[...]).astype(BF16)
    o = jnp.concatenate([oa_ref[...], nb, oc_sc[0].astype(BF16), oc_sc[1].astype(BF16)], axis=1)
    mix = jnp.dot(o, wo_ref[...], preferred_element_type=F32)
    x = _layer_norm(ALPHA * x_ref[...] + mix, g2_ref[...], b2_ref[...])
    o_ref[...] = _ffn_ln_tile(x, wg_ref, wu_ref, wd_ref, g3_ref, b3_ref)


def _mix_ffn(oa2, ob2, oc2, x2d, gb, wo, g2, b2, wg, wu, wd, g3, b3, l, tm):
    tok = lambda rows, width: pl.BlockSpec((rows, width), lambda i: (i, 0))
    return pl.pallas_call(
        _mix_ffn_kernel,
        grid=(x2d.shape[0] // tm,),
        in_specs=[tok(tm, D_A), tok(tm, D_B), tok(tm // S5_CHUNK, S5_CHUNK * D_C), tok(tm, D_MODEL)]
                 + [_layer_spec(a, l) for a in (gb, wo, g2, b2, wg, wu, wd, g3, b3)],
        out_specs=tok(tm, D_MODEL),
        out_shape=jax.ShapeDtypeStruct(x2d.shape, F32),
        scratch_shapes=[pltpu.VMEM((2, tm, 128), F32)],
        compiler_params=_cparams("parallel"),
        name="mix_ffn_ln",
    )(oa2, ob2, oc2, x2d, gb, wo, g2, b2, wg, wu, wd, g3, b3)


def _block_diag(w):
    eye = jnp.eye(A_BLOCKS, dtype=w.dtype)
    return jnp.einsum('lhij,hk->lhikj', w, eye).reshape(w.shape[0], D_A, D_A)


def _pick(n, pref):
    t = min(n, pref)
    while n % t:
        t //= 2
    return t


def kernel(x, ffn1_w_gate, ffn1_w_up, ffn1_w_down, ln1_g, ln1_b, w_in, conv_w, conv_b, rg_w_a, rg_b_a, rg_w_x, rg_b_x, rg_lambda, fox_b_f, s5_a_re, s5_a_im, s5_log_dt, s5_b_re, s5_b_im, s5_c_re, s5_c_im, s5_d, s5_w_glu, mix_norm_g, w_out, ln2_g, ln2_b, ffn2_w_gate, ffn2_w_up, ffn2_w_down, ln3_g, ln3_b):
    bsz, seq, _ = x.shape
    n = bsz * seq
    tm = _pick(seq, 512)
    vec = lambda v: v[:, None, :]
    bf = lambda w: w.astype(BF16)

    f1, f2 = [bf(w) for w in (ffn1_w_gate, ffn1_w_up, ffn1_w_down)], [bf(w) for w in (ffn2_w_gate, ffn2_w_up, ffn2_w_down)]
    w16 = bf(w_in)
    cuts = np.cumsum([D_A, D_A, D_B, D_B, D_B, B_HEADS])
    place = np.zeros((B_HEADS, 128), np.float32)
    for piece in range(3):
        place[np.arange(B_HEADS), F_LANE0 + 8 * piece + np.arange(B_HEADS)] = 1.0
    place = jnp.asarray(place)
    fcols = jnp.einsum('ldh,hn->ldn', w16[:, :, cuts[4]:cuts[5]], place.astype(BF16))
    wb = jnp.concatenate([w16[:, :, W_SPLIT:cuts[4]], w16[:, :, cuts[5]:], fcols], axis=-1)
    bf_row = jnp.einsum('lh,hn->ln', fox_b_f, place, precision=lax.Precision.HIGHEST)[:, None, :]
    fmask = jnp.max(place, axis=0, keepdims=True)
    wgates = bf(jnp.concatenate([_block_diag(rg_w_a), _block_diag(rg_w_x)], axis=-1))
    bgates = jnp.concatenate([rg_b_a, rg_b_x], axis=-1)[:, None, :]
    g_a, g_b, g_c = (vec(g) for g in jnp.split(mix_norm_g, [D_A, D_A + D_B], axis=-1))
    sp = vec(jax.nn.softplus(-rg_lambda))
    m, bm, cm, lev, pw = _s5_prepare(s5_a_re, s5_a_im, s5_log_dt, s5_b_re, s5_b_im, s5_c_re, s5_c_im)
    wglu, wo = bf(s5_w_glu), bf(w_out)

    h = x.reshape(n, D_MODEL)
    for l in range(DEPTH):
        h = _ffn_ln(h, f1[0], f1[1], f1[2], vec(ln1_g), vec(ln1_b), l, tm)
        oa, qa, ka, v, cu, base = _in_proj(h, w16, wb, bf_row, fmask, place, conv_w, vec(conv_b), wgates, bgates,
                                           sp, g_a, l, bsz, seq, tm)
        ob = _fox(qa, ka, v.reshape(bsz, seq, D_B), base, tm)
        oc = _s5(cu, m, bm, cm, lev, pw, vec(s5_d), wglu, g_c, l, seq // S5_CHUNK)
        h = _mix_ffn(oa, ob.reshape(n, D_B), oc, h, g_b, wo, vec(ln2_g), vec(ln2_b), f2[0], f2[1], f2[2],
                     vec(ln3_g), vec(ln3_b), l, tm)
    return h.reshape(bsz, seq, D_MODEL)
```

```python
import functools
import math

import numpy as np
import jax
import jax.numpy as jnp
from jax import lax
from jax.experimental import pallas as pl
from jax.experimental.pallas import tpu as pltpu

F32 = jnp.float32
BF16 = jnp.bfloat16

D_MODEL = 1024
D_FF = 2816
D_A = 384
A_BLOCKS = 6
A_BLOCK_DIM = 64
CONV_WIDTH = 4
RG_C = 8.0
D_B = 384
B_HEADS = 6
B_HEAD_DIM = 64
HEAD_PAIRS = B_HEADS // 2
D_C = 256
C_GROUP = 16
C_GROUPS = 16
C_STATE = 64
W_SPLIT = 1792
W_TAIL = 2 * D_A + 3 * D_B - W_SPLIT + D_C + 128
F_LANE0 = 64
LOG2E = math.log2(math.e)
QSCALE = LOG2E * B_HEAD_DIM ** -0.5
NEG = -1e30
SKIP_BELOW = -200.0
NORM_GUARD = 1.0 + 2.0 ** -7
ROWS = 128
FFN_ROWS = 256
DEPTH = 2
ALPHA = (2 * DEPTH) ** 0.25
LN_EPS = 1e-5
RMS_EPS = 1e-6

S5_CHUNK = 8
S5_HALF = C_GROUPS // 2
S5_HCOLS = S5_CHUNK * S5_HALF * C_GROUP
S5_HSTATE = S5_HALF * C_STATE

VMEM_LIMIT = 56 * 1024 * 1024


def _cparams(*sem):
    return pltpu.CompilerParams(dimension_semantics=sem, vmem_limit_bytes=VMEM_LIMIT)


def _const_spec(shape):
    nd = len(shape)
    return pl.BlockSpec(shape, lambda *_: (0,) * nd, pipeline_mode=pl.Buffered(1))


def _layer_spec(arr, l, block=None):
    block = tuple(arr.shape[1:]) if block is None else block
    nd = len(block)
    return pl.BlockSpec((None,) + block, lambda *_: (l,) + (0,) * nd, pipeline_mode=pl.Buffered(1))


def _layer_norm(r, g, b):
    mu = jnp.mean(r, axis=-1, keepdims=True)
    d = r - mu
    var = jnp.mean(d * d, axis=-1, keepdims=True)
    return d * lax.rsqrt(var + LN_EPS) * g + b


def _rms_norm(x, g):
    return x * lax.rsqrt(jnp.mean(x * x, axis=-1, keepdims=True) + RMS_EPS) * g


def _gelu(x):
    return 0.5 * x * (1.0 + jnp.tanh(math.sqrt(2.0 / math.pi) * (x + 0.044715 * (x * x * x))))


def _log_sigmoid(x):
    return -(jnp.maximum(-x, 0.0) + jnp.log1p(jnp.exp(-jnp.abs(x))))


def _shift_rows(x, d):
    rows, cols = x.shape
    groups = rows // 8
    x3 = x.reshape(groups, 8, cols)
    if d % 8 == 0:
        return jnp.concatenate([jnp.zeros((d // 8, 8, cols), x.dtype), x3[:groups - d // 8]], axis=0).reshape(rows, cols)
    sub = lax.broadcasted_iota(jnp.int32, (groups, 8, cols), 1)
    rolled = pltpu.roll(x3, d, axis=1)
    before = jnp.concatenate([jnp.zeros((1, 8, cols), x.dtype), rolled[:groups - 1]], axis=0)
    return jnp.where(sub >= d, rolled, before).reshape(rows, cols)


def _ffn_ln_tile(x, wg_ref, wu_ref, wd_ref, g_ref, b_ref):
    xb = x.astype(BF16)
    gate = jnp.dot(xb, wg_ref[...], preferred_element_type=F32)
    up = jnp.dot(xb, wu_ref[...], preferred_element_type=F32)
    h = (gate * jax.nn.sigmoid(gate) * up).astype(BF16)
    y = jnp.dot(h, wd_ref[...], preferred_element_type=F32)
    return _layer_norm(ALPHA * x + 0.5 * y, g_ref[...], b_ref[...])


def _ffn_ln_kernel(x_ref, wg_ref, wu_ref, wd_ref, g_ref, b_ref, o_ref):
    for c in range(x_ref.shape[0] // FFN_ROWS):
        rows = slice(c * FFN_ROWS, (c + 1) * FFN_ROWS)
        o_ref[rows, :] = _ffn_ln_tile(x_ref[rows, :], wg_ref, wu_ref, wd_ref, g_ref, b_ref)


def _ffn_ln(x2d, wg, wu, wd, g, b, l, tm):
    n = x2d.shape[0]
    row = pl.BlockSpec((tm, D_MODEL), lambda i: (i, 0))
    return pl.pallas_call(
        _ffn_ln_kernel,
        grid=(n // tm,),
        in_specs=[row] + [_layer_spec(a, l) for a in (wg, wu, wd, g, b)],
        out_specs=row,
        out_shape=jax.ShapeDtypeStruct((n, D_MODEL), F32),
        compiler_params=_cparams("parallel"),
        name="ffn_ln",
    )(x2d, wg, wu, wd, g, b)


def _rglru_gates(ax, cw_ref, cb_ref, wg_ref, bg_ref, prev8_sc):
    tm = ax.shape[0]
    groups = tm // 8
    sub = lax.broadcasted_iota(jnp.int32, (groups, 8, D_A), 1)
    ax3 = ax.reshape(groups, 8, D_A)
    prev8 = prev8_sc[...]
    xa = cb_ref[...] + cw_ref[CONV_WIDTH - 1:CONV_WIDTH, :] * ax3
    for k in range(1, CONV_WIDTH):
        rolled = pltpu.roll(ax3, k, axis=1)
        before = jnp.concatenate([pltpu.roll(prev8, k, axis=0)[None], rolled[:groups - 1]], axis=0)
        xa = xa + cw_ref[CONV_WIDTH - 1 - k:CONV_WIDTH - k, :] * jnp.where(sub >= k, rolled, before)
    prev8_sc[...] = ax3[groups - 1]
    xa = xa.reshape(tm, D_A)
    return xa, jnp.dot(xa.astype(BF16), wg_ref[...], preferred_element_type=F32) + bg_ref[...]


def _rglru_scan(xa, gates, ag, sp_ref, gn_ref, hcarry_sc, hs_sc):
    tm = xa.shape[0]
    groups = tm // 8
    grouped = lambda v: v.reshape(groups, 8, D_A)
    sub = lax.broadcasted_iota(jnp.int32, (groups, 8, D_A), 1)
    r = jax.nn.sigmoid(gates[:, :D_A])
    i = jax.nn.sigmoid(gates[:, D_A:])
    log_a = (-RG_C) * r * sp_ref[...]
    a = jnp.exp(log_a)
    y = -jnp.tanh(log_a) * (1.0 + a * a)
    b = jnp.where(y > 0.0, y * lax.rsqrt(y), 0.0) * (i * xa)

    a3, b3 = grouped(a), grouped(b)
    for d in (1, 2, 4):
        valid = sub >= d
        b3 = jnp.where(valid, a3 * pltpu.roll(b3, d, axis=1) + b3, b3)
        a3 = jnp.where(valid, a3 * pltpu.roll(a3, d, axis=1), a3)
    last = hcarry_sc[...]
    for g in range(groups):
        hg = b3[g] + a3[g] * last
        hs_sc[8 * g:8 * g + 8, :] = hg
        last = hg[7:8, :]
    hcarry_sc[...] = last
    return _rms_norm(_gelu(ag) * hs_sc[...], gn_ref[...])


def _in_proj_kernel(x_ref, wa_ref, wb_ref, bf_ref, fmask_ref, ones_ref, hsel_ref, cw_ref, cb_ref, wg_ref, bg_ref, sp_ref,
                    gn_ref, oa_ref, qa_ref, ka_ref, v_ref, cu_ref, base_ref, skip_ref,
                    carry_sc, prev8_sc, hcarry_sc, hs_sc, cu_sc, hist_sc):
    tm = x_ref.shape[0]
    step = pl.program_id(1)

    @pl.when(step == 0)
    def _():
        carry_sc[...] = jnp.zeros_like(carry_sc)
        prev8_sc[...] = jnp.zeros_like(prev8_sc)
        hcarry_sc[...] = jnp.zeros_like(hcarry_sc)
        hist_sc[...] = jnp.zeros_like(hist_sc)

    xb = x_ref[...].astype(BF16)
    o = 2 * D_A
    v_rest = W_SPLIT - (o + 2 * D_B)
    zb = jnp.dot(xb, wb_ref[...], preferred_element_type=F32)
    za = jnp.dot(xb, wa_ref[:, 0:o], preferred_element_type=F32)

    cu = zb[:, D_B - v_rest:D_B - v_rest + D_C]
    cu_sc[0] = cu[:, :128]
    cu_sc[1] = cu[:, 128:]
    cu_ref[...] = jnp.concatenate([cu_sc[j, pl.ds(t, tm // S5_CHUNK, stride=S5_CHUNK), :]
                                   for t in range(S5_CHUNK) for j in range(2)], axis=1)

    lf = _log_sigmoid(zb[:, W_TAIL - 128:] + bf_ref[...]) * fmask_ref[...] * LOG2E
    lane = lax.broadcasted_iota(jnp.int32, (tm, 128), 1)
    c = lf
    d = 1
    while d < tm:
        c = c + _shift_rows(c, d)
        d *= 2
    carry = carry_sc[...]
    rows = [jnp.broadcast_to(carry[:, F_LANE0 + h:F_LANE0 + h + 1], (1, 128)) for h in range(B_HEADS)]
    pad = jnp.zeros((6, 128), F32)
    base_ref[0, 0] = jnp.stack([jnp.concatenate([rows[2 * p], rows[2 * p + 1], pad], axis=0)
                                for p in range(HEAD_PAIRS)])
    carry_sc[...] = carry + c[tm - 1:tm, :]
    hi = c.astype(BF16).astype(F32)
    r1 = c - hi
    mid = r1.astype(BF16).astype(F32)
    lo = (r1 - mid).astype(BF16).astype(F32)
    pieces = jnp.where(lane < F_LANE0 + 8, hi, jnp.where(lane < F_LANE0 + 16, mid, lo))

    xa, gates = _rglru_gates(za[:, 0:D_A], cw_ref, cb_ref, wg_ref, bg_ref, prev8_sc)
    zq = jnp.dot(xb, wa_ref[:, o:W_SPLIT], preferred_element_type=F32)
    oa_ref[...] = _rglru_scan(xa, gates, za[:, D_A:], sp_ref, gn_ref, hcarry_sc, hs_sc).astype(BF16)

    v_ref[...] = jnp.concatenate([zq[:, 2 * D_B:], zb[:, :D_B - v_rest]], axis=1).astype(BF16)
    sqs = []
    for p in range(HEAD_PAIRS):
        qt = zq[:, 128 * p:128 * (p + 1)]
        kt = zq[:, D_B + 128 * p:D_B + 128 * (p + 1)]
        sqs += [qt * qt, kt * kt]
        for e in range(2):
            h = 2 * p + e
            qh = pltpu.roll(qt, B_HEAD_DIM, axis=1) if e else qt
            kh = pltpu.roll(kt, B_HEAD_DIM, axis=1) if e else kt
            qa_ref[0, h] = jnp.where(lane < B_HEAD_DIM, qh * QSCALE, ones_ref[h:h + 1, :]).astype(BF16)
            ka_ref[0, h] = jnp.where(lane < B_HEAD_DIM, kh, -pieces).astype(BF16)
    norms = jnp.dot(jnp.concatenate(sqs, axis=1).astype(BF16), hsel_ref[...], preferred_element_type=F32)
    top = jnp.max(norms, axis=0, keepdims=True)
    qn2 = top
    kn2 = pltpu.roll(top, 120, axis=1)

    qn = jnp.sqrt(qn2) * (QSCALE * NORM_GUARD)
    kn_here = jnp.sqrt(kn2)
    alive = jnp.ones((1, 128), F32)
    count = jnp.zeros((1, 128), F32)
    for kb in range(hist_sc.shape[0]):
        bound = qn * (jnp.sqrt(hist_sc[kb, 1:2, :]) + kn_here) + (carry - hist_sc[kb, 0:1, :])
        ok = jnp.where(bound <= SKIP_BELOW, 1.0, 0.0)
        ok = ok * pltpu.roll(ok, 127, axis=1)
        alive = alive * jnp.where(kb < step, ok, 0.0)
        count = count + alive
    skip_ref[0, 0] = jnp.broadcast_to(count, (8, 128))
    hist_sc[step, 0:1, :] = carry + c[tm - 1:tm, :]
    hist_sc[step, 1:2, :] = kn2


def _in_proj(x2d, w16, wb, bf_row, fmask, ones, hsel, cw, cb, wg, bg, sp, gn, l, bsz, seq, tm):
    ns = seq // tm
    n = bsz * seq
    tok = lambda rows, width: pl.BlockSpec((rows, width), lambda b, s: (b * ns + s, 0))
    heads = pl.BlockSpec((1, B_HEADS, tm, 128), lambda b, s: (b, 0, s, 0))
    return pl.pallas_call(
        _in_proj_kernel,
        grid=(bsz, ns),
        in_specs=[tok(tm, D_MODEL), _layer_spec(w16, l, (D_MODEL, W_SPLIT)), _layer_spec(wb, l),
                  _layer_spec(bf_row, l), _const_spec((1, 128)), _const_spec((B_HEADS, 128)), _const_spec(hsel.shape)]
                 + [_layer_spec(a, l) for a in (cw, cb, wg, bg, sp, gn)],
        out_specs=[tok(tm, D_A), heads, heads, tok(tm, D_B), tok(tm // S5_CHUNK, S5_CHUNK * D_C),
                   pl.BlockSpec((1, 1, HEAD_PAIRS, 8, 128), lambda b, s: (b, s, 0, 0, 0)),
                   pl.BlockSpec((1, 1, 8, 128), lambda b, s: (b, s, 0, 0))],
        out_shape=[jax.ShapeDtypeStruct((n, D_A), BF16),
                   jax.ShapeDtypeStruct((bsz, B_HEADS, seq, 128), BF16),
                   jax.ShapeDtypeStruct((bsz, B_HEADS, seq, 128), BF16),
                   jax.ShapeDtypeStruct((n, D_B), BF16),
                   jax.ShapeDtypeStruct((n // S5_CHUNK, S5_CHUNK * D_C), F32),
                   jax.ShapeDtypeStruct((bsz, ns, HEAD_PAIRS, 8, 128), F32),
                   jax.ShapeDtypeStruct((bsz, ns, 8, 128), F32)],
        scratch_shapes=[pltpu.VMEM((1, 128), F32), pltpu.VMEM((8, D_A), F32), pltpu.VMEM((1, D_A), F32),
                        pltpu.VMEM((tm, D_A), F32), pltpu.VMEM((2, tm, 128), F32), pltpu.VMEM((ns, 8, 128), F32)],
        compiler_params=_cparams("parallel", "arbitrary"),
        name="in_proj",
    )(x2d, w16, wb, bf_row, fmask, ones, hsel, cw, cb, wg, bg, sp, gn)


def _fox_kernel(skip_ref, qa_ref, ka_ref, v_ref, base_ref, o_ref, vaug_sc, acc_sc, m_sc, *, tq):
    qi = pl.program_id(2)
    first = skip_ref[pl.program_id(0), pl.program_id(1), qi]
    past = qi - first
    seq = v_ref.shape[1]
    lane = lax.broadcasted_iota(jnp.int32, (ROWS, 128), 1)
    sub = lax.broadcasted_iota(jnp.int32, (ROWS, 128), 0)

    @pl.when(qi == 0)
    def _():
        lane_v = lax.broadcasted_iota(jnp.int32, (tq, 128), 1)
        for c in range(seq // tq):
            vc = v_ref[0, c * tq:(c + 1) * tq, :].astype(F32)
            vaug_sc[0, c * tq:(c + 1) * tq, :] = jnp.concatenate(
                [jnp.where(lane_v < B_HEAD_DIM, vc, 0.0), jnp.where(lane_v == 0, 1.0, 0.0)], axis=1).astype(BF16)
            vaug_sc[1, c * tq:(c + 1) * tq, :] = jnp.concatenate(
                [jnp.where(lane_v >= B_HEAD_DIM, vc, 0.0), jnp.where(lane_v == 1, 1.0, 0.0)], axis=1).astype(BF16)

    acc_sc[...] = jnp.zeros_like(acc_sc)
    m_sc[...] = jnp.full_like(m_sc, NEG)
    base_q = base_ref[0, qi, 0]

    def tasks_of(k0, base_k, diag):
        deltas = [base_q[j:j + 1, :] - base_k[j:j + 1, :] for j in range(2)]
        return [(k0, deltas, diag, r) for r in range(tq // ROWS)]

    def qk(task):
        k0, _, diag, r = task
        n = (r + 1) * ROWS if diag else tq
        return [lax.dot_general(qa_ref[0, j, r * ROWS:(r + 1) * ROWS, :], ka_ref[0, j, pl.ds(k0, n), :],
                                (((1,), (1,)), ((), ())), preferred_element_type=F32) for j in range(2)]

    def finish(task, s_pair):
        k0, deltas, diag, r = task
        rows = slice(r * ROWS, (r + 1) * ROWS)
        n = (r + 1) * ROWS if diag else tq
        ps, alphas = [], []
        for j in range(2):
            s = s_pair[j]
            tiles = [s[:, c * 128:(c + 1) * 128] for c in range(n // 128)]
            if diag:
                tiles[-1] = jnp.where(lane <= sub, tiles[-1], NEG)
            tmax = tiles[0]
            for t in tiles[1:]:
                tmax = jnp.maximum(tmax, t)
            m_cur = jnp.max(tmax, axis=-1, keepdims=True) + deltas[j]
            m_prev = m_sc[j, rows, :]
            m_new = jnp.maximum(m_prev, m_cur)
            alphas.append(jnp.exp2(m_prev - m_new))
            m_sc[j, rows, :] = m_new
            mt = m_new - deltas[j]
            ps.append(jnp.concatenate([jnp.exp2(t - mt) for t in tiles], axis=1).astype(BF16))
        vk = jnp.concatenate([vaug_sc[0, pl.ds(k0, n), :], vaug_sc[1, pl.ds(k0, n), :]], axis=0)
        pv = jnp.dot(jnp.concatenate(ps, axis=1), vk, preferred_element_type=F32)
        a = jnp.concatenate([jnp.where(lane < B_HEAD_DIM, alphas[0], alphas[1]),
                             jnp.where(lane == 0, alphas[0], alphas[1])], axis=1)
        acc_sc[rows, :] = a * acc_sc[rows, :] + pv

    def run(tasks):
        s_next = qk(tasks[0])
        for i, task in enumerate(tasks):
            s_cur = s_next
            if i + 1 < len(tasks):
                s_next = qk(tasks[i + 1])
            finish(task, s_cur)

    def past_tasks(ki):
        return tasks_of(pl.multiple_of(ki * tq, tq), base_ref[0, ki, 0], False)

    def body(i, carry):
        run([t for u in range(4) for t in past_tasks(first + 4 * i + u)])
        return carry

    lax.fori_loop(0, past // 4, body, 0)
    rem0 = first + (past // 4) * 4

    for tail in range(4):
        @pl.when(past % 4 == tail)
        def _(tail=tail):
            run([t for u in range(tail) for t in past_tasks(rem0 + u)]
                + tasks_of(pl.multiple_of(qi * tq, tq), base_q, True))
            acc = acc_sc[...]
            lane_o = lax.broadcasted_iota(jnp.int32, (tq, 128), 1)
            o_ref[0] = acc[:, :128] / jnp.where(lane_o < B_HEAD_DIM, acc[:, 128:129], acc[:, 129:130])


def _fox(skip, qa, ka, v3, base, tq):
    bsz, seq, _ = v3.shape
    ns = seq // tq
    return pl.pallas_call(
        functools.partial(_fox_kernel, tq=tq),
        grid_spec=pltpu.PrefetchScalarGridSpec(
            num_scalar_prefetch=1,
            grid=(bsz, HEAD_PAIRS, ns),
            in_specs=[pl.BlockSpec((1, 2, tq, 128), lambda b, h, i, sk: (b, h, i, 0)),
                      pl.BlockSpec((1, 2, seq, 128), lambda b, h, i, sk: (b, h, 0, 0)),
                      pl.BlockSpec((1, seq, 128), lambda b, h, i, sk: (b, 0, h)),
                      pl.BlockSpec((1, ns, 1, 8, 128), lambda b, h, i, sk: (b, 0, h, 0, 0))],
            out_specs=pl.BlockSpec((1, tq, 128), lambda b, h, i, sk: (b, i, h)),
            scratch_shapes=[pltpu.VMEM((2, seq, 256), BF16), pltpu.VMEM((tq, 256), F32),
                            pltpu.VMEM((2, tq, 128), F32)]),
        out_shape=jax.ShapeDtypeStruct((bsz, seq, D_B), F32),
        compiler_params=_cparams("parallel", "parallel", "arbitrary"),
        name="fox_attention",
    )(skip, qa, ka, v3, base)


def _s5_kernel(u_ref, m_ref, bm_ref, cm_ref, lev_ref, pw_ref, d_ref, wglu_ref, gn_ref, o_ref, hs_sc):
    u = u_ref[...]
    rows = u.shape[0]
    groups = rows // 8
    ub = u.astype(BF16)
    sub8 = lax.broadcasted_iota(jnp.int32, (8, S5_HSTATE), 0)
    y_halves = []
    for h in range(2):
        re_cols = slice((2 * h) * S5_HSTATE, (2 * h + 1) * S5_HSTATE)
        im_cols = slice((2 * h + 1) * S5_HSTATE, (2 * h + 2) * S5_HSTATE)
        uh = jnp.concatenate([ub[:, t * D_C + h * 128: t * D_C + (h + 1) * 128] for t in range(S5_CHUNK)],
                             axis=1)
        y = jnp.dot(uh, m_ref[h], preferred_element_type=F32)
        st = jnp.dot(uh, bm_ref[h], preferred_element_type=F32)
        re3 = st[:, :S5_HSTATE].reshape(groups, 8, S5_HSTATE)
        im3 = st[:, S5_HSTATE:].reshape(groups, 8, S5_HSTATE)
        for lv, d in enumerate((1, 2, 4)):
            ar = jnp.where(sub8 >= d, lev_ref[lv:lv + 1, re_cols], 0.0)
            ai = jnp.where(sub8 >= d, lev_ref[lv:lv + 1, im_cols], 0.0)
            rr = pltpu.roll(re3, d, axis=1)
            ri = pltpu.roll(im3, d, axis=1)
            re3, im3 = re3 + (ar * rr - ai * ri), im3 + (ar * ri + ai * rr)
        pr, pi = pw_ref[:, re_cols], pw_ref[:, im_cols]
        cre = jnp.zeros((1, S5_HSTATE), F32)
        cim = jnp.zeros((1, S5_HSTATE), F32)
        for g in range(groups):
            hr = re3[g] + (pr * cre - pi * cim)
            hi = im3[g] + (pr * cim + pi * cre)
            hs_sc[0, 8 * g:8 * g + 8, :] = hr
            hs_sc[1, 8 * g:8 * g + 8, :] = hi
            cre, cim = hr[7:8, :], hi[7:8, :]
        hprev = jnp.concatenate([_shift_rows(hs_sc[0], 1), _shift_rows(hs_sc[1], 1)], axis=1)
        y = y + jnp.dot(hprev.astype(BF16), cm_ref[h], preferred_element_type=F32)
        y_halves.append(y)
    outs = []
    for t in range(S5_CHUNK):
        yt = jnp.concatenate([y_halves[0][:, t * 128:(t + 1) * 128], y_halves[1][:, t * 128:(t + 1) * 128]], axis=1)
        yt = _gelu(yt + d_ref[...] * u[:, t * D_C:(t + 1) * D_C])
        gl = jnp.dot(yt.astype(BF16), wglu_ref[...], preferred_element_type=F32)
        outs.append(_rms_norm(yt * jax.nn.sigmoid(gl), gn_ref[...]))
    o_ref[...] = jnp.concatenate(outs, axis=1)


def _s5(u2, m, bm, cm, lev, pw, d, wglu, gn, l, rows_per_seq):
    n, width = u2.shape
    blk = pl.BlockSpec((rows_per_seq, width), lambda b: (b, 0))
    return pl.pallas_call(
        _s5_kernel,
        grid=(n // rows_per_seq,),
        in_specs=[blk] + [_layer_spec(a, l) for a in (m, bm, cm, lev, pw, d, wglu, gn)],
        out_specs=blk,
        out_shape=jax.ShapeDtypeStruct(u2.shape, F32),
        scratch_shapes=[pltpu.VMEM((2, rows_per_seq, S5_HSTATE), F32)],
        compiler_params=_cparams("parallel"),
        name="s5",
    )(u2, m, bm, cm, lev, pw, d, wglu, gn)


def _s5_prepare(a_re, a_im, log_dt, b_re, b_im, c_re, c_im):
    hi = lax.Precision.HIGHEST
    T = S5_CHUNK
    nl = a_re.shape[0]
    dt = jnp.exp(log_dt)[:, :, None]
    lr, li = a_re * dt, a_im * dt

    def cpow(tau):
        mag = jnp.exp(lr[:, None] * tau[None, :, None, None])
        ang = li[:, None] * tau[None, :, None, None]
        return mag * jnp.cos(ang), mag * jnp.sin(ang)

    def halves(z):
        return jnp.moveaxis(z.reshape(z.shape[:2] + (2, S5_HALF) + z.shape[3:]), 2, 1)

    ar1, ai1 = cpow(jnp.ones((1,), F32))
    nr, ni = ar1[:, 0] - 1.0, ai1[:, 0]
    den = a_re * a_re + a_im * a_im
    fr, fi = (nr * a_re + ni * a_im) / den, (ni * a_re - nr * a_im) / den
    bbr = fr[..., None] * b_re - fi[..., None] * b_im
    bbi = fr[..., None] * b_im + fi[..., None] * b_re

    dr, di = cpow((T - 1) - jnp.arange(T, dtype=F32))
    abr = dr[..., None] * bbr[:, None] - di[..., None] * bbi[:, None]
    abi = dr[..., None] * bbi[:, None] + di[..., None] * bbr[:, None]

    def same_group(rows, rdiv, cols, cdiv, cmod=S5_HALF):
        r = (jnp.arange(rows) // rdiv)[:, None]
        c = ((jnp.arange(cols) // cdiv) % cmod)[None, :]
        return (r == c).astype(F32)

    def tile_eye(n, reps):
        return (jnp.arange(n)[:, None] == (jnp.arange(n * reps) % n)[None, :]).astype(F32)

    kern = (jnp.einsum('lgop,ljgpi->ljgio', c_re, abr, precision=hi)
            - jnp.einsum('lgop,ljgpi->ljgio', c_im, abi, precision=hi))
    kd = jnp.einsum('lhjrc,cn->lhjrn', halves(kern).reshape(nl, 2, T, 128, C_GROUP), tile_eye(C_GROUP, S5_HALF))
    kd = kd * same_group(128, C_GROUP, 128, C_GROUP)
    ext = jnp.concatenate([kd, jnp.zeros((nl, 2, T - 1, 128, 128), F32)], axis=2)
    m = jnp.concatenate([ext[:, :, T - 1 - b:2 * T - 1 - b] for b in range(T)], axis=-1)
    m = m.reshape(nl, 2, S5_HCOLS, S5_HCOLS)

    def state_in(ab):
        y = jnp.swapaxes(halves(ab), -1, -2).reshape(nl, 2, T, 128, C_STATE)
        return jnp.einsum('lhjrp,pn->lhjrn', y, tile_eye(C_STATE, S5_HALF)) * same_group(128, C_GROUP, S5_HSTATE, C_STATE)

    bm = jnp.concatenate([state_in(abr), state_in(abi)], axis=-1).reshape(nl, 2, S5_HCOLS, 2 * S5_HSTATE)

    ur, ui = cpow(jnp.arange(1, T + 1, dtype=F32))
    so_r = c_re[:, None] * ur[:, :, :, None, :] - c_im[:, None] * ui[:, :, :, None, :]
    so_i = c_re[:, None] * ui[:, :, :, None, :] + c_im[:, None] * ur[:, :, :, None, :]
    expand = (jnp.arange(T * C_GROUP)[:, None] // C_GROUP == jnp.arange(S5_HCOLS)[None, :] // 128) & (
        jnp.arange(T * C_GROUP)[:, None] % C_GROUP == jnp.arange(S5_HCOLS)[None, :] % C_GROUP)

    def state_out(so):
        y = jnp.transpose(halves(so), (0, 1, 3, 5, 2, 4)).reshape(nl, 2, S5_HSTATE, T * C_GROUP)
        return jnp.einsum('lhrc,cn->lhrn', y, expand.astype(F32)) * same_group(S5_HSTATE, C_STATE, S5_HCOLS, C_GROUP)

    cm = jnp.concatenate([state_out(so_r), state_out(-so_i)], axis=2)

    def state_layout(zr, zi):
        n = zr.shape[1]
        parts = []
        for h in range(2):
            gs = slice(h * S5_HALF, (h + 1) * S5_HALF)
            parts += [zr[:, :, gs].reshape(nl, n, S5_HSTATE), zi[:, :, gs].reshape(nl, n, S5_HSTATE)]
        return jnp.concatenate(parts, axis=-1)

    lev = state_layout(*cpow(jnp.asarray([T, 2 * T, 4 * T], F32)))
    pw = state_layout(*cpow(T * jnp.arange(1, 9, dtype=F32)))
    return m.astype(BF16), bm.astype(BF16), cm.astype(BF16), lev, pw


def _mix_ffn_kernel(oa_ref, ob_ref, oc_ref, x_ref, gb_ref, wo_ref, g2_ref, b2_ref, wg_ref, wu_ref, wd_ref,
                    g3_ref, b3_ref, o_ref, oc_sc):
    tm = x_ref.shape[0]
    for t in range(S5_CHUNK):
        for j in range(2):
            oc_sc[j, pl.ds(t, tm // S5_CHUNK, stride=S5_CHUNK), :] = oc_ref[:, t * D_C + j * 128:t * D_C + (j + 1) * 128]
    nb = _rms_norm(ob_ref[...], gb_ref[...]).astype(BF16)
    o = jnp.concatenate([oa_ref[...], nb, oc_sc[0].astype(BF16), oc_sc[1].astype(BF16)], axis=1)
    mix = jnp.dot(o, wo_ref[...], preferred_element_type=F32)
    x = _layer_norm(ALPHA * x_ref[...] + mix, g2_ref[...], b2_ref[...])
    o_ref[...] = _ffn_ln_tile(x, wg_ref, wu_ref, wd_ref, g3_ref, b3_ref)


def _mix_ffn(oa2, ob2, oc2, x2d, gb, wo, g2, b2, wg, wu, wd, g3, b3, l, tm):
    tok = lambda rows, width: pl.BlockSpec((rows, width), lambda i: (i, 0))
    return pl.pallas_call(
        _mix_ffn_kernel,
        grid=(x2d.shape[0] // tm,),
        in_specs=[tok(tm, D_A), tok(tm, D_B), tok(tm // S5_CHUNK, S5_CHUNK * D_C), tok(tm, D_MODEL)]
                 + [_layer_spec(a, l) for a in (gb, wo, g2, b2, wg, wu, wd, g3, b3)],
        out_specs=tok(tm, D_MODEL),
        out_shape=jax.ShapeDtypeStruct(x2d.shape, F32),
        scratch_shapes=[pltpu.VMEM((2, tm, 128), F32)],
        compiler_params=_cparams("parallel"),
        name="mix_ffn_ln",
    )(oa2, ob2, oc2, x2d, gb, wo, g2, b2, wg, wu, wd, g3, b3)


def _block_diag(w):
    eye = jnp.eye(A_BLOCKS, dtype=w.dtype)
    return jnp.einsum('lhij,hk->lhikj', w, eye).reshape(w.shape[0], D_A, D_A)


def _pick(n, pref):
    t = min(n, pref)
    while n % t:
        t //= 2
    return t


def kernel(x, ffn1_w_gate, ffn1_w_up, ffn1_w_down, ln1_g, ln1_b, w_in, conv_w, conv_b, rg_w_a, rg_b_a, rg_w_x, rg_b_x, rg_lambda, fox_b_f, s5_a_re, s5_a_im, s5_log_dt, s5_b_re, s5_b_im, s5_c_re, s5_c_im, s5_d, s5_w_glu, mix_norm_g, w_out, ln2_g, ln2_b, ffn2_w_gate, ffn2_w_up, ffn2_w_down, ln3_g, ln3_b):
    bsz, seq, _ = x.shape
    n = bsz * seq
    tm = _pick(seq, 512)
    vec = lambda v: v[:, None, :]
    bf = lambda w: w.astype(BF16)

    f1, f2 = [bf(w) for w in (ffn1_w_gate, ffn1_w_up, ffn1_w_down)], [bf(w) for w in (ffn2_w_gate, ffn2_w_up, ffn2_w_down)]
    w16 = bf(w_in)
    cuts = np.cumsum([D_A, D_A, D_B, D_B, D_B, B_HEADS])
    place = np.zeros((B_HEADS, 128), np.float32)
    for piece in range(3):
        place[np.arange(B_HEADS), F_LANE0 + 8 * piece + np.arange(B_HEADS)] = 1.0
    place = jnp.asarray(place)
    hsel = np.zeros((2 * HEAD_PAIRS * 128, 128), np.float32)
    for p in range(HEAD_PAIRS):
        for is_k in range(2):
            for e in range(2):
                r0 = (2 * p + is_k) * 128 + e * B_HEAD_DIM
                hsel[r0:r0 + B_HEAD_DIM, F_LANE0 + 8 * is_k + 2 * p + e] = 1.0
    hsel = jnp.asarray(hsel, BF16)
    fcols = jnp.einsum('ldh,hn->ldn', w16[:, :, cuts[4]:cuts[5]], place.astype(BF16))
    wb = jnp.concatenate([w16[:, :, W_SPLIT:cuts[4]], w16[:, :, cuts[5]:], fcols], axis=-1)
    bf_row = jnp.einsum('lh,hn->ln', fox_b_f, place, precision=lax.Precision.HIGHEST)[:, None, :]
    fmask = jnp.max(place, axis=0, keepdims=True)
    wgates = bf(jnp.concatenate([_block_diag(rg_w_a), _block_diag(rg_w_x)], axis=-1))
    bgates = jnp.concatenate([rg_b_a, rg_b_x], axis=-1)[:, None, :]
    g_a, g_b, g_c = (vec(g) for g in jnp.split(mix_norm_g, [D_A, D_A + D_B], axis=-1))
    sp = vec(jax.nn.softplus(-rg_lambda))
    m, bm, cm, lev, pw = _s5_prepare(s5_a_re, s5_a_im, s5_log_dt, s5_b_re, s5_b_im, s5_c_re, s5_c_im)
    wglu, wo = bf(s5_w_glu), bf(w_out)

    h = x.reshape(n, D_MODEL)
    for l in range(DEPTH):
        h = _ffn_ln(h, f1[0], f1[1], f1[2], vec(ln1_g), vec(ln1_b), l, tm)
        oa, qa, ka, v, cu, base, skip = _in_proj(h, w16, wb, bf_row, fmask, place, hsel, conv_w, vec(conv_b), wgates,
                                                 bgates, sp, g_a, l, bsz, seq, tm)
        skip = jnp.transpose(skip[:, :, 0, F_LANE0:F_LANE0 + B_HEADS:2], (0, 2, 1)).astype(jnp.int32)
        ob = _fox(skip, qa, ka, v.reshape(bsz, seq, D_B), base, tm)
        oc = _s5(cu, m, bm, cm, lev, pw, vec(s5_d), wglu, g_c, l, seq // S5_CHUNK)
        h = _mix_ffn(oa, ob.reshape(n, D_B), oc, h, g_b, wo, vec(ln2_g), vec(ln2_b), f2[0], f2[1], f2[2],
                     vec(ln3_g), vec(ln3_b), l, tm)
    return h.reshape(bsz, seq, D_MODEL)
```

```python
import functools
import math

import numpy as np
import jax
import jax.numpy as jnp
from jax import lax
from jax.experimental import pallas as pl
from jax.experimental.pallas import tpu as pltpu

F32 = jnp.float32
BF16 = jnp.bfloat16

D_MODEL = 1024
D_FF = 2816
D_A = 384
A_BLOCKS = 6
A_BLOCK_DIM = 64
CONV_WIDTH = 4
RG_C = 8.0
D_B = 384
B_HEADS = 6
B_HEAD_DIM = 64
HEAD_PAIRS = B_HEADS // 2
D_C = 256
C_GROUP = 16
C_GROUPS = 16
C_STATE = 64
W_SPLIT = 1792
W_TAIL = 2 * D_A + 3 * D_B - W_SPLIT + D_C + 128
F_LANE0 = 64
LOG2E = math.log2(math.e)
QSCALE = LOG2E * B_HEAD_DIM ** -0.5
NEG = -1e30
SKIP_BELOW = -200.0
NORM_GUARD = 1.0 + 2.0 ** -7
ROWS = 128
FFN_ROWS = 256
DEPTH = 2
ALPHA = (2 * DEPTH) ** 0.25
LN_EPS = 1e-5
RMS_EPS = 1e-6

S5_CHUNK = 8
S5_HALF = C_GROUPS // 2
S5_HCOLS = S5_CHUNK * S5_HALF * C_GROUP
S5_HSTATE = S5_HALF * C_STATE

VMEM_LIMIT = 56 * 1024 * 1024


def _cparams(*sem):
    return pltpu.CompilerParams(dimension_semantics=sem, vmem_limit_bytes=VMEM_LIMIT)


def _const_spec(shape):
    nd = len(shape)
    return pl.BlockSpec(shape, lambda *_: (0,) * nd, pipeline_mode=pl.Buffered(1))


def _layer_spec(arr, l, block=None):
    block = tuple(arr.shape[1:]) if block is None else block
    nd = len(block)
    return pl.BlockSpec((None,) + block, lambda *_: (l,) + (0,) * nd, pipeline_mode=pl.Buffered(1))


def _layer_norm(r, g, b):
    mu = jnp.mean(r, axis=-1, keepdims=True)
    d = r - mu
    var = jnp.mean(d * d, axis=-1, keepdims=True)
    return d * lax.rsqrt(var + LN_EPS) * g + b


def _rms_norm(x, g):
    return x * lax.rsqrt(jnp.mean(x * x, axis=-1, keepdims=True) + RMS_EPS) * g


def _gelu(x):
    return 0.5 * x * (1.0 + jnp.tanh(math.sqrt(2.0 / math.pi) * (x + 0.044715 * (x * x * x))))


def _log_sigmoid(x):
    return -(jnp.maximum(-x, 0.0) + jnp.log1p(jnp.exp(-jnp.abs(x))))


def _shift_rows(x, d):
    rows, cols = x.shape
    groups = rows // 8
    x3 = x.reshape(groups, 8, cols)
    if d % 8 == 0:
        return jnp.concatenate([jnp.zeros((d // 8, 8, cols), x.dtype), x3[:groups - d // 8]], axis=0).reshape(rows, cols)
    sub = lax.broadcasted_iota(jnp.int32, (groups, 8, cols), 1)
    rolled = pltpu.roll(x3, d, axis=1)
    before = jnp.concatenate([jnp.zeros((1, 8, cols), x.dtype), rolled[:groups - 1]], axis=0)
    return jnp.where(sub >= d, rolled, before).reshape(rows, cols)


def _ffn_ln_tile(x, wg_ref, wu_ref, wd_ref, g_ref, b_ref):
    xb = x.astype(BF16)
    gate = jnp.dot(xb, wg_ref[...], preferred_element_type=F32)
    up = jnp.dot(xb, wu_ref[...], preferred_element_type=F32)
    h = (gate * jax.nn.sigmoid(gate) * up).astype(BF16)
    y = jnp.dot(h, wd_ref[...], preferred_element_type=F32)
    return _layer_norm(ALPHA * x + 0.5 * y, g_ref[...], b_ref[...])


def _ffn_ln_kernel(x_ref, wg_ref, wu_ref, wd_ref, g_ref, b_ref, o_ref):
    for c in range(x_ref.shape[0] // FFN_ROWS):
        rows = slice(c * FFN_ROWS, (c + 1) * FFN_ROWS)
        o_ref[rows, :] = _ffn_ln_tile(x_ref[rows, :], wg_ref, wu_ref, wd_ref, g_ref, b_ref)


def _ffn_ln(x2d, wg, wu, wd, g, b, l, tm):
    n = x2d.shape[0]
    row = pl.BlockSpec((tm, D_MODEL), lambda i: (i, 0))
    return pl.pallas_call(
        _ffn_ln_kernel,
        grid=(n // tm,),
        in_specs=[row] + [_layer_spec(a, l) for a in (wg, wu, wd, g, b)],
        out_specs=row,
        out_shape=jax.ShapeDtypeStruct((n, D_MODEL), F32),
        compiler_params=_cparams("parallel"),
        name="ffn_ln",
    )(x2d, wg, wu, wd, g, b)


def _rglru_gates(ax, cw_ref, cb_ref, wg_ref, bg_ref, prev8_sc):
    tm = ax.shape[0]
    groups = tm // 8
    sub = lax.broadcasted_iota(jnp.int32, (groups, 8, D_A), 1)
    ax3 = ax.reshape(groups, 8, D_A)
    prev8 = prev8_sc[...]
    xa = cb_ref[...] + cw_ref[CONV_WIDTH - 1:CONV_WIDTH, :] * ax3
    for k in range(1, CONV_WIDTH):
        rolled = pltpu.roll(ax3, k, axis=1)
        before = jnp.concatenate([pltpu.roll(prev8, k, axis=0)[None], rolled[:groups - 1]], axis=0)
        xa = xa + cw_ref[CONV_WIDTH - 1 - k:CONV_WIDTH - k, :] * jnp.where(sub >= k, rolled, before)
    prev8_sc[...] = ax3[groups - 1]
    xa = xa.reshape(tm, D_A)
    return xa, jnp.dot(xa.astype(BF16), wg_ref[...], preferred_element_type=F32) + bg_ref[...]


def _rglru_scan(xa, gates, ag, sp_ref, gn_ref, hcarry_sc, hs_sc):
    tm = xa.shape[0]
    groups = tm // 8
    grouped = lambda v: v.reshape(groups, 8, D_A)
    sub = lax.broadcasted_iota(jnp.int32, (groups, 8, D_A), 1)
    r = jax.nn.sigmoid(gates[:, :D_A])
    i = jax.nn.sigmoid(gates[:, D_A:])
    log_a = (-RG_C) * r * sp_ref[...]
    a = jnp.exp(log_a)
    y = -jnp.tanh(log_a) * (1.0 + a * a)
    b = jnp.where(y > 0.0, y * lax.rsqrt(y), 0.0) * (i * xa)

    a3, b3 = grouped(a), grouped(b)
    for d in (1, 2, 4):
        valid = sub >= d
        b3 = jnp.where(valid, a3 * pltpu.roll(b3, d, axis=1) + b3, b3)
        a3 = jnp.where(valid, a3 * pltpu.roll(a3, d, axis=1), a3)
    last = hcarry_sc[...]
    for g in range(groups):
        hg = b3[g] + a3[g] * last
        hs_sc[8 * g:8 * g + 8, :] = hg
        last = hg[7:8, :]
    hcarry_sc[...] = last
    return _rms_norm(_gelu(ag) * hs_sc[...], gn_ref[...])


def _in_proj_kernel(x_ref, wa_ref, wb_ref, bf_ref, fmask_ref, ones_ref, hsel_ref, cw_ref, cb_ref, wg_ref, bg_ref, sp_ref,
                    gn_ref, oa_ref, qa_ref, ka_ref, v_ref, cu_ref, base_ref, skip_ref,
                    carry_sc, prev8_sc, hcarry_sc, hs_sc, cu_sc, hist_sc):
    tm = x_ref.shape[0]
    step = pl.program_id(1)

    @pl.when(step == 0)
    def _():
        carry_sc[...] = jnp.zeros_like(carry_sc)
        prev8_sc[...] = jnp.zeros_like(prev8_sc)
        hcarry_sc[...] = jnp.zeros_like(hcarry_sc)
        hist_sc[...] = jnp.zeros_like(hist_sc)

    xb = x_ref[...].astype(BF16)
    o = 2 * D_A
    v_rest = W_SPLIT - (o + 2 * D_B)
    zb = jnp.dot(xb, wb_ref[...], preferred_element_type=F32)
    za = jnp.dot(xb, wa_ref[:, 0:o], preferred_element_type=F32)

    cu = zb[:, D_B - v_rest:D_B - v_rest + D_C]
    cu_sc[0] = cu[:, :128]
    cu_sc[1] = cu[:, 128:]
    cu_ref[...] = jnp.concatenate([cu_sc[j, pl.ds(t, tm // S5_CHUNK, stride=S5_CHUNK), :]
                                   for t in range(S5_CHUNK) for j in range(2)], axis=1)

    lf = _log_sigmoid(zb[:, W_TAIL - 128:] + bf_ref[...]) * fmask_ref[...] * LOG2E
    lane = lax.broadcasted_iota(jnp.int32, (tm, 128), 1)
    c = lf
    d = 1
    while d < tm:
        c = c + _shift_rows(c, d)
        d *= 2
    carry = carry_sc[...]
    rows = [jnp.broadcast_to(carry[:, F_LANE0 + h:F_LANE0 + h + 1], (1, 128)) for h in range(B_HEADS)]
    pad = jnp.zeros((6, 128), F32)
    base_ref[0, 0] = jnp.stack([jnp.concatenate([rows[2 * p], rows[2 * p + 1], pad], axis=0)
                                for p in range(HEAD_PAIRS)])
    carry_sc[...] = carry + c[tm - 1:tm, :]
    hi = c.astype(BF16).astype(F32)
    r1 = c - hi
    mid = r1.astype(BF16).astype(F32)
    lo = (r1 - mid).astype(BF16).astype(F32)
    pieces = jnp.where(lane < F_LANE0 + 8, hi, jnp.where(lane < F_LANE0 + 16, mid, lo))

    xa, gates = _rglru_gates(za[:, 0:D_A], cw_ref, cb_ref, wg_ref, bg_ref, prev8_sc)
    zq = jnp.dot(xb, wa_ref[:, o:W_SPLIT], preferred_element_type=F32)
    oa_ref[...] = _rglru_scan(xa, gates, za[:, D_A:], sp_ref, gn_ref, hcarry_sc, hs_sc).astype(BF16)

    v_ref[...] = jnp.concatenate([zq[:, 2 * D_B:], zb[:, :D_B - v_rest]], axis=1).astype(BF16)
    sqs = []
    for p in range(HEAD_PAIRS):
        qt = zq[:, 128 * p:128 * (p + 1)]
        kt = zq[:, D_B + 128 * p:D_B + 128 * (p + 1)]
        sqs += [qt * qt, kt * kt]
        for e in range(2):
            h = 2 * p + e
            qh = pltpu.roll(qt, B_HEAD_DIM, axis=1) if e else qt
            kh = pltpu.roll(kt, B_HEAD_DIM, axis=1) if e else kt
            qa_ref[0, h] = jnp.where(lane < B_HEAD_DIM, qh * QSCALE, ones_ref[h:h + 1, :]).astype(BF16)
            ka_ref[0, h] = jnp.where(lane < B_HEAD_DIM, kh, -pieces).astype(BF16)
    norms = jnp.dot(jnp.concatenate(sqs, axis=1).astype(BF16), hsel_ref[...], preferred_element_type=F32)
    top = jnp.max(norms, axis=0, keepdims=True)
    qn2 = top
    kn2 = pltpu.roll(top, 120, axis=1)

    qn = jnp.sqrt(qn2) * (QSCALE * NORM_GUARD)
    kn_here = jnp.sqrt(kn2)
    alive = jnp.ones((1, 128), F32)
    count = jnp.zeros((1, 128), F32)
    for kb in range(hist_sc.shape[0]):
        bound = qn * (jnp.sqrt(hist_sc[kb, 1:2, :]) + kn_here) + (carry - hist_sc[kb, 0:1, :])
        ok = jnp.where(bound <= SKIP_BELOW, 1.0, 0.0)
        ok = ok * pltpu.roll(ok, 127, axis=1)
        alive = alive * jnp.where(kb < step, ok, 0.0)
        count = count + alive
    skip_ref[0, 0] = jnp.broadcast_to(count, (8, 128))
    hist_sc[step, 0:1, :] = carry + c[tm - 1:tm, :]
    hist_sc[step, 1:2, :] = kn2


def _in_proj(x2d, wa, wb, bf_row, fmask, ones, hsel, cw, cb, wg, bg, sp, gn, l, bsz, seq, tm):
    ns = seq // tm
    n = bsz * seq
    tok = lambda rows, width: pl.BlockSpec((rows, width), lambda b, s: (b * ns + s, 0))
    heads = pl.BlockSpec((1, B_HEADS, tm, 128), lambda b, s: (b, 0, s, 0))
    return pl.pallas_call(
        _in_proj_kernel,
        grid=(bsz, ns),
        in_specs=[tok(tm, D_MODEL), _layer_spec(wa, l), _layer_spec(wb, l),
                  _layer_spec(bf_row, l), _const_spec((1, 128)), _const_spec((B_HEADS, 128)), _const_spec(hsel.shape)]
                 + [_layer_spec(a, l) for a in (cw, cb, wg, bg, sp, gn)],
        out_specs=[tok(tm, D_A), heads, heads, tok(tm, D_B), tok(tm // S5_CHUNK, S5_CHUNK * D_C),
                   pl.BlockSpec((1, 1, HEAD_PAIRS, 8, 128), lambda b, s: (b, s, 0, 0, 0)),
                   pl.BlockSpec((1, 1, 8, 128), lambda b, s: (b, s, 0, 0))],
        out_shape=[jax.ShapeDtypeStruct((n, D_A), BF16),
                   jax.ShapeDtypeStruct((bsz, B_HEADS, seq, 128), BF16),
                   jax.ShapeDtypeStruct((bsz, B_HEADS, seq, 128), BF16),
                   jax.ShapeDtypeStruct((n, D_B), BF16),
                   jax.ShapeDtypeStruct((n // S5_CHUNK, S5_CHUNK * D_C), F32),
                   jax.ShapeDtypeStruct((bsz, ns, HEAD_PAIRS, 8, 128), F32),
                   jax.ShapeDtypeStruct((bsz, ns, 8, 128), F32)],
        scratch_shapes=[pltpu.VMEM((1, 128), F32), pltpu.VMEM((8, D_A), F32), pltpu.VMEM((1, D_A), F32),
                        pltpu.VMEM((tm, D_A), F32), pltpu.VMEM((2, tm, 128), F32), pltpu.VMEM((ns, 8, 128), F32)],
        compiler_params=_cparams("parallel", "arbitrary"),
        name="in_proj",
    )(x2d, wa, wb, bf_row, fmask, ones, hsel, cw, cb, wg, bg, sp, gn)


def _fox_kernel(skip_ref, qa_ref, ka_ref, v_ref, base_ref, o_ref, vaug_sc, acc_sc, m_sc, *, tq):
    seq = v_ref.shape[1]
    lane = lax.broadcasted_iota(jnp.int32, (ROWS, 128), 1)
    sub = lax.broadcasted_iota(jnp.int32, (ROWS, 128), 0)

    lane_v = lax.broadcasted_iota(jnp.int32, (tq, 128), 1)
    for c in range(seq // tq):
        vc = v_ref[0, c * tq:(c + 1) * tq, :].astype(F32)
        vaug_sc[0, c * tq:(c + 1) * tq, :] = jnp.concatenate(
            [jnp.where(lane_v < B_HEAD_DIM, vc, 0.0), jnp.where(lane_v == 0, 1.0, 0.0)], axis=1).astype(BF16)
        vaug_sc[1, c * tq:(c + 1) * tq, :] = jnp.concatenate(
            [jnp.where(lane_v >= B_HEAD_DIM, vc, 0.0), jnp.where(lane_v == 1, 1.0, 0.0)], axis=1).astype(BF16)

    def query_block(qi, carry):
        first = skip_ref[pl.program_id(0), pl.program_id(1), qi]
        past = qi - first
        q0 = pl.multiple_of(qi * tq, tq)
        acc_sc[...] = jnp.zeros_like(acc_sc)
        m_sc[...] = jnp.full_like(m_sc, NEG)
        base_q = base_ref[0, qi, 0]

        def tasks_of(k0, base_k, diag):
            deltas = [base_q[j:j + 1, :] - base_k[j:j + 1, :] for j in range(2)]
            return [(k0, deltas, diag, r) for r in range(tq // ROWS)]

        def qk(task):
            k0, _, diag, r = task
            n = (r + 1) * ROWS if diag else tq
            rows = pl.ds(pl.multiple_of(q0 + r * ROWS, ROWS), ROWS)
            return [lax.dot_general(qa_ref[0, j, rows, :], ka_ref[0, j, pl.ds(k0, n), :],
                                    (((1,), (1,)), ((), ())), preferred_element_type=F32) for j in range(2)]

        def finish(task, s_pair):
            k0, deltas, diag, r = task
            rows = slice(r * ROWS, (r + 1) * ROWS)
            n = (r + 1) * ROWS if diag else tq
            ps, alphas = [], []
            for j in range(2):
                s = s_pair[j]
                tiles = [s[:, c * 128:(c + 1) * 128] for c in range(n // 128)]
                if diag:
                    tiles[-1] = jnp.where(lane <= sub, tiles[-1], NEG)
                tmax = tiles[0]
                for t in tiles[1:]:
                    tmax = jnp.maximum(tmax, t)
                m_cur = jnp.max(tmax, axis=-1, keepdims=True) + deltas[j]
                m_prev = m_sc[j, rows, :]
                m_new = jnp.maximum(m_prev, m_cur)
                alphas.append(jnp.exp2(m_prev - m_new))
                m_sc[j, rows, :] = m_new
                mt = m_new - deltas[j]
                ps.append(jnp.concatenate([jnp.exp2(t - mt) for t in tiles], axis=1).astype(BF16))
            vk = jnp.concatenate([vaug_sc[0, pl.ds(k0, n), :], vaug_sc[1, pl.ds(k0, n), :]], axis=0)
            pv = jnp.dot(jnp.concatenate(ps, axis=1), vk, preferred_element_type=F32)
            a = jnp.concatenate([jnp.where(lane < B_HEAD_DIM, alphas[0], alphas[1]),
                                 jnp.where(lane == 0, alphas[0], alphas[1])], axis=1)
            acc_sc[rows, :] = a * acc_sc[rows, :] + pv

        def run(tasks):
            s_next = qk(tasks[0])
            for i, task in enumerate(tasks):
                s_cur = s_next
                if i + 1 < len(tasks):
                    s_next = qk(tasks[i + 1])
                finish(task, s_cur)

        def past_tasks(ki):
            return tasks_of(pl.multiple_of(ki * tq, tq), base_ref[0, ki, 0], False)

        def quad(i, c):
            run([t for u in range(4) for t in past_tasks(first + 4 * i + u)])
            return c

        lax.fori_loop(0, past // 4, quad, 0)
        rem0 = first + (past // 4) * 4

        for tail in range(4):
            @pl.when(past % 4 == tail)
            def _(tail=tail):
                run([t for u in range(tail) for t in past_tasks(rem0 + u)] + tasks_of(q0, base_q, True))
                acc = acc_sc[...]
                lane_o = lax.broadcasted_iota(jnp.int32, (tq, 128), 1)
                o_ref[0, pl.ds(q0, tq), :] = acc[:, :128] / jnp.where(lane_o < B_HEAD_DIM, acc[:, 128:129], acc[:, 129:130])

        return carry

    lax.fori_loop(0, seq // tq, query_block, 0)


def _fox(skip, qa, ka, v3, base, tq):
    bsz, seq, _ = v3.shape
    ns = seq // tq
    heads = pl.BlockSpec((1, 2, seq, 128), lambda b, h, sk: (b, h, 0, 0))
    return pl.pallas_call(
        functools.partial(_fox_kernel, tq=tq),
        grid_spec=pltpu.PrefetchScalarGridSpec(
            num_scalar_prefetch=1,
            grid=(bsz, HEAD_PAIRS),
            in_specs=[heads, heads,
                      pl.BlockSpec((1, seq, 128), lambda b, h, sk: (b, 0, h)),
                      pl.BlockSpec((1, ns, 1, 8, 128), lambda b, h, sk: (b, 0, h, 0, 0))],
            out_specs=pl.BlockSpec((1, seq, 128), lambda b, h, sk: (b, 0, h)),
            scratch_shapes=[pltpu.VMEM((2, seq, 256), BF16), pltpu.VMEM((tq, 256), F32),
                            pltpu.VMEM((2, tq, 128), F32)]),
        out_shape=jax.ShapeDtypeStruct((bsz, seq, D_B), F32),
        compiler_params=_cparams("parallel", "parallel"),
        name="fox_attention",
    )(skip, qa, ka, v3, base)


def _s5_kernel(u_ref, m_ref, bm_ref, cm_ref, lev_ref, pw_ref, d_ref, wglu_ref, gn_ref, o_ref, hs_sc):
    u = u_ref[...]
    rows = u.shape[0]
    groups = rows // 8
    ub = u.astype(BF16)
    sub8 = lax.broadcasted_iota(jnp.int32, (8, S5_HSTATE), 0)
    y_halves = []
    for h in range(2):
        re_cols = slice((2 * h) * S5_HSTATE, (2 * h + 1) * S5_HSTATE)
        im_cols = slice((2 * h + 1) * S5_HSTATE, (2 * h + 2) * S5_HSTATE)
        uh = jnp.concatenate([ub[:, t * D_C + h * 128: t * D_C + (h + 1) * 128] for t in range(S5_CHUNK)],
                             axis=1)
        y = jnp.dot(uh, m_ref[h], preferred_element_type=F32)
        st = jnp.dot(uh, bm_ref[h], preferred_element_type=F32)
        re3 = st[:, :S5_HSTATE].reshape(groups, 8, S5_HSTATE)
        im3 = st[:, S5_HSTATE:].reshape(groups, 8, S5_HSTATE)
        for lv, d in enumerate((1, 2, 4)):
            ar = jnp.where(sub8 >= d, lev_ref[lv:lv + 1, re_cols], 0.0)
            ai = jnp.where(sub8 >= d, lev_ref[lv:lv + 1, im_cols], 0.0)
            rr = pltpu.roll(re3, d, axis=1)
            ri = pltpu.roll(im3, d, axis=1)
            re3, im3 = re3 + (ar * rr - ai * ri), im3 + (ar * ri + ai * rr)
        pr, pi = pw_ref[:, re_cols], pw_ref[:, im_cols]
        cre = jnp.zeros((1, S5_HSTATE), F32)
        cim = jnp.zeros((1, S5_HSTATE), F32)
        for g in range(groups):
            hr = re3[g] + (pr * cre - pi * cim)
            hi = im3[g] + (pr * cim + pi * cre)
            hs_sc[0, 8 * g:8 * g + 8, :] = hr
            hs_sc[1, 8 * g:8 * g + 8, :] = hi
            cre, cim = hr[7:8, :], hi[7:8, :]
        hprev = jnp.concatenate([_shift_rows(hs_sc[0], 1), _shift_rows(hs_sc[1], 1)], axis=1)
        y = y + jnp.dot(hprev.astype(BF16), cm_ref[h], preferred_element_type=F32)
        y_halves.append(y)
    outs = []
    for t in range(S5_CHUNK):
        yt = jnp.concatenate([y_halves[0][:, t * 128:(t + 1) * 128], y_halves[1][:, t * 128:(t + 1) * 128]], axis=1)
        yt = _gelu(yt + d_ref[...] * u[:, t * D_C:(t + 1) * D_C])
        gl = jnp.dot(yt.astype(BF16), wglu_ref[...], preferred_element_type=F32)
        outs.append(_rms_norm(yt * jax.nn.sigmoid(gl), gn_ref[...]))
    o_ref[...] = jnp.concatenate(outs, axis=1)


def _s5(u2, m, bm, cm, lev, pw, d, wglu, gn, l, rows_per_seq):
    n, width = u2.shape
    blk = pl.BlockSpec((rows_per_seq, width), lambda b: (b, 0))
    return pl.pallas_call(
        _s5_kernel,
        grid=(n // rows_per_seq,),
        in_specs=[blk] + [_layer_spec(a, l) for a in (m, bm, cm, lev, pw, d, wglu, gn)],
        out_specs=blk,
        out_shape=jax.ShapeDtypeStruct(u2.shape, F32),
        scratch_shapes=[pltpu.VMEM((2, rows_per_seq, S5_HSTATE), F32)],
        compiler_params=_cparams("parallel"),
        name="s5",
    )(u2, m, bm, cm, lev, pw, d, wglu, gn)


def _s5_prepare(a_re, a_im, log_dt, b_re, b_im, c_re, c_im):
    hi = lax.Precision.HIGHEST
    T = S5_CHUNK
    nl = a_re.shape[0]
    dt = jnp.exp(log_dt)[:, :, None]
    lr, li = a_re * dt, a_im * dt

    def cpow(tau):
        mag = jnp.exp(lr[:, None] * tau[None, :, None, None])
        ang = li[:, None] * tau[None, :, None, None]
        return mag * jnp.cos(ang), mag * jnp.sin(ang)

    def halves(z):
        return jnp.moveaxis(z.reshape(z.shape[:2] + (2, S5_HALF) + z.shape[3:]), 2, 1)

    ar1, ai1 = cpow(jnp.ones((1,), F32))
    nr, ni = ar1[:, 0] - 1.0, ai1[:, 0]
    den = a_re * a_re + a_im * a_im
    fr, fi = (nr * a_re + ni * a_im) / den, (ni * a_re - nr * a_im) / den
    bbr = fr[..., None] * b_re - fi[..., None] * b_im
    bbi = fr[..., None] * b_im + fi[..., None] * b_re

    dr, di = cpow((T - 1) - jnp.arange(T, dtype=F32))
    abr = dr[..., None] * bbr[:, None] - di[..., None] * bbi[:, None]
    abi = dr[..., None] * bbi[:, None] + di[..., None] * bbr[:, None]

    def same_group(rows, rdiv, cols, cdiv, cmod=S5_HALF):
        r = (jnp.arange(rows) // rdiv)[:, None]
        c = ((jnp.arange(cols) // cdiv) % cmod)[None, :]
        return (r == c).astype(F32)

    def tile_eye(n, reps):
        return (jnp.arange(n)[:, None] == (jnp.arange(n * reps) % n)[None, :]).astype(F32)

    kern = (jnp.einsum('lgop,ljgpi->ljgio', c_re, abr, precision=hi)
            - jnp.einsum('lgop,ljgpi->ljgio', c_im, abi, precision=hi))
    kd = jnp.einsum('lhjrc,cn->lhjrn', halves(kern).reshape(nl, 2, T, 128, C_GROUP), tile_eye(C_GROUP, S5_HALF))
    kd = kd * same_group(128, C_GROUP, 128, C_GROUP)
    ext = jnp.concatenate([kd, jnp.zeros((nl, 2, T - 1, 128, 128), F32)], axis=2)
    m = jnp.concatenate([ext[:, :, T - 1 - b:2 * T - 1 - b] for b in range(T)], axis=-1)
    m = m.reshape(nl, 2, S5_HCOLS, S5_HCOLS)

    def state_in(ab):
        y = jnp.swapaxes(halves(ab), -1, -2).reshape(nl, 2, T, 128, C_STATE)
        return jnp.einsum('lhjrp,pn->lhjrn', y, tile_eye(C_STATE, S5_HALF)) * same_group(128, C_GROUP, S5_HSTATE, C_STATE)

    bm = jnp.concatenate([state_in(abr), state_in(abi)], axis=-1).reshape(nl, 2, S5_HCOLS, 2 * S5_HSTATE)

    ur, ui = cpow(jnp.arange(1, T + 1, dtype=F32))
    so_r = c_re[:, None] * ur[:, :, :, None, :] - c_im[:, None] * ui[:, :, :, None, :]
    so_i = c_re[:, None] * ui[:, :, :, None, :] + c_im[:, None] * ur[:, :, :, None, :]
    expand = (jnp.arange(T * C_GROUP)[:, None] // C_GROUP == jnp.arange(S5_HCOLS)[None, :] // 128) & (
        jnp.arange(T * C_GROUP)[:, None] % C_GROUP == jnp.arange(S5_HCOLS)[None, :] % C_GROUP)

    def state_out(so):
        y = jnp.transpose(halves(so), (0, 1, 3, 5, 2, 4)).reshape(nl, 2, S5_HSTATE, T * C_GROUP)
        return jnp.einsum('lhrc,cn->lhrn', y, expand.astype(F32)) * same_group(S5_HSTATE, C_STATE, S5_HCOLS, C_GROUP)

    cm = jnp.concatenate([state_out(so_r), state_out(-so_i)], axis=2)

    def state_layout(zr, zi):
        n = zr.shape[1]
        parts = []
        for h in range(2):
            gs = slice(h * S5_HALF, (h + 1) * S5_HALF)
            parts += [zr[:, :, gs].reshape(nl, n, S5_HSTATE), zi[:, :, gs].reshape(nl, n, S5_HSTATE)]
        return jnp.concatenate(parts, axis=-1)

    lev = state_layout(*cpow(jnp.asarray([T, 2 * T, 4 * T], F32)))
    pw = state_layout(*cpow(T * jnp.arange(1, 9, dtype=F32)))
    return m.astype(BF16), bm.astype(BF16), cm.astype(BF16), lev, pw


def _mix_ffn_kernel(oa_ref, ob_ref, oc_ref, x_ref, gb_ref, wo_ref, g2_ref, b2_ref, wg_ref, wu_ref, wd_ref,
                    g3_ref, b3_ref, o_ref, oc_sc):
    tm = x_ref.shape[0]
    for t in range(S5_CHUNK):
        for j in range(2):
            oc_sc[j, pl.ds(t, tm // S5_CHUNK, stride=S5_CHUNK), :] = oc_ref[:, t * D_C + j * 128:t * D_C + (j + 1) * 128]
    nb = _rms_norm(ob_ref[...], gb_ref[...]).astype(BF16)
    o = jnp.concatenate([oa_ref[...], nb, oc_sc[0].astype(BF16), oc_sc[1].astype(BF16)], axis=1)
    mix = jnp.dot(o, wo_ref[...], preferred_element_type=F32)
    x = _layer_norm(ALPHA * x_ref[...] + mix, g2_ref[...], b2_ref[...])
    o_ref[...] = _ffn_ln_tile(x, wg_ref, wu_ref, wd_ref, g3_ref, b3_ref)


def _mix_ffn(oa2, ob2, oc2, x2d, gb, wo, g2, b2, wg, wu, wd, g3, b3, l, tm):
    tok = lambda rows, width: pl.BlockSpec((rows, width), lambda i: (i, 0))
    return pl.pallas_call(
        _mix_ffn_kernel,
        grid=(x2d.shape[0] // tm,),
        in_specs=[tok(tm, D_A), tok(tm, D_B), tok(tm // S5_CHUNK, S5_CHUNK * D_C), tok(tm, D_MODEL)]
                 + [_layer_spec(a, l) for a in (gb, wo, g2, b2, wg, wu, wd, g3, b3)],
        out_specs=tok(tm, D_MODEL),
        out_shape=jax.ShapeDtypeStruct(x2d.shape, F32),
        scratch_shapes=[pltpu.VMEM((2, tm, 128), F32)],
        compiler_params=_cparams("parallel"),
        name="mix_ffn_ln",
    )(oa2, ob2, oc2, x2d, gb, wo, g2, b2, wg, wu, wd, g3, b3)


def _block_diag(w):
    eye = jnp.eye(A_BLOCKS, dtype=w.dtype)
    return jnp.einsum('lhij,hk->lhikj', w, eye).reshape(w.shape[0], D_A, D_A)


def _pick(n, pref):
    t = min(n, pref)
    while n % t:
        t //= 2
    return t


def kernel(x, ffn1_w_gate, ffn1_w_up, ffn1_w_down, ln1_g, ln1_b, w_in, conv_w, conv_b, rg_w_a, rg_b_a, rg_w_x, rg_b_x, rg_lambda, fox_b_f, s5_a_re, s5_a_im, s5_log_dt, s5_b_re, s5_b_im, s5_c_re, s5_c_im, s5_d, s5_w_glu, mix_norm_g, w_out, ln2_g, ln2_b, ffn2_w_gate, ffn2_w_up, ffn2_w_down, ln3_g, ln3_b):
    bsz, seq, _ = x.shape
    n = bsz * seq
    tm = _pick(seq, 512)
    vec = lambda v: v[:, None, :]
    bf = lambda w: w.astype(BF16)

    f1, f2 = [bf(w) for w in (ffn1_w_gate, ffn1_w_up, ffn1_w_down)], [bf(w) for w in (ffn2_w_gate, ffn2_w_up, ffn2_w_down)]
    cuts = np.cumsum([D_A, D_A, D_B, D_B, D_B, B_HEADS])
    wa = bf(w_in[:, :, :W_SPLIT])
    wt = bf(w_in[:, :, W_SPLIT:])
    place = np.zeros((B_HEADS, 128), np.float32)
    for piece in range(3):
        place[np.arange(B_HEADS), F_LANE0 + 8 * piece + np.arange(B_HEADS)] = 1.0
    place = jnp.asarray(place)
    hsel = np.zeros((2 * HEAD_PAIRS * 128, 128), np.float32)
    for p in range(HEAD_PAIRS):
        for is_k in range(2):
            for e in range(2):
                r0 = (2 * p + is_k) * 128 + e * B_HEAD_DIM
                hsel[r0:r0 + B_HEAD_DIM, F_LANE0 + 8 * is_k + 2 * p + e] = 1.0
    hsel = jnp.asarray(hsel, BF16)
    fcols = jnp.einsum('ldh,hn->ldn', wt[:, :, cuts[4] - W_SPLIT:cuts[5] - W_SPLIT], place.astype(BF16))
    wb = jnp.concatenate([wt[:, :, :cuts[4] - W_SPLIT], wt[:, :, cuts[5] - W_SPLIT:], fcols], axis=-1)
    bf_row = jnp.einsum('lh,hn->ln', fox_b_f, place, precision=lax.Precision.HIGHEST)[:, None, :]
    fmask = jnp.max(place, axis=0, keepdims=True)
    wgates = bf(jnp.concatenate([_block_diag(rg_w_a), _block_diag(rg_w_x)], axis=-1))
    bgates = jnp.concatenate([rg_b_a, rg_b_x], axis=-1)[:, None, :]
    g_a, g_b, g_c = (vec(g) for g in jnp.split(mix_norm_g, [D_A, D_A + D_B], axis=-1))
    sp = vec(jax.nn.softplus(-rg_lambda))
    m, bm, cm, lev, pw = _s5_prepare(s5_a_re, s5_a_im, s5_log_dt, s5_b_re, s5_b_im, s5_c_re, s5_c_im)
    wglu, wo = bf(s5_w_glu), bf(w_out)

    h = x.reshape(n, D_MODEL)
    for l in range(DEPTH):
        h = _ffn_ln(h, f1[0], f1[1], f1[2], vec(ln1_g), vec(ln1_b), l, tm)
        oa, qa, ka, v, cu, base, skip = _in_proj(h, wa, wb, bf_row, fmask, place, hsel, conv_w, vec(conv_b), wgates,
                                                 bgates, sp, g_a, l, bsz, seq, tm)
        skip = jnp.transpose(skip[:, :, 0, F_LANE0:F_LANE0 + B_HEADS:2], (0, 2, 1)).astype(jnp.int32)
        ob = _fox(skip, qa, ka, v.reshape(bsz, seq, D_B), base, tm)
        oc = _s5(cu, m, bm, cm, lev, pw, vec(s5_d), wglu, g_c, l, seq // S5_CHUNK)
        h = _mix_ffn(oa, ob.reshape(n, D_B), oc, h, g_b, wo, vec(ln2_g), vec(ln2_b), f2[0], f2[1], f2[2],
                     vec(ln3_g), vec(ln3_b), l, tm)
    return h.reshape(bsz, seq, D_MODEL)
```

```python
import functools
import math

import numpy as np
import jax
import jax.numpy as jnp
from jax import lax
from jax.experimental import pallas as pl
from jax.experimental.pallas import tpu as pltpu

F32 = jnp.float32
BF16 = jnp.bfloat16

D_MODEL = 1024
D_FF = 2816
D_A = 384
A_BLOCKS = 6
A_BLOCK_DIM = 64
CONV_WIDTH = 4
RG_C = 8.0
D_B = 384
B_HEADS = 6
B_HEAD_DIM = 64
HEAD_PAIRS = B_HEADS // 2
D_C = 256
C_GROUP = 16
C_GROUPS = 16
C_STATE = 64
W_SPLIT = 1792
W_TAIL = 2 * D_A + 3 * D_B - W_SPLIT + D_C + 128
F_LANE0 = 64
LOG2E = math.log2(math.e)
QSCALE = LOG2E * B_HEAD_DIM ** -0.5
NEG = -1e30
SKIP_BELOW = -200.0
NORM_GUARD = 1.0 + 2.0 ** -7
ROWS = 128
FFN_ROWS = 256
DEPTH = 2
ALPHA = (2 * DEPTH) ** 0.25
LN_EPS = 1e-5
RMS_EPS = 1e-6

S5_CHUNK = 8
S5_HALF = C_GROUPS // 2
S5_HCOLS = S5_CHUNK * S5_HALF * C_GROUP
S5_HSTATE = S5_HALF * C_STATE

VMEM_LIMIT = 56 * 1024 * 1024


def _cparams(*sem):
    return pltpu.CompilerParams(dimension_semantics=sem, vmem_limit_bytes=VMEM_LIMIT)


def _const_spec(shape):
    nd = len(shape)
    return pl.BlockSpec(shape, lambda *_: (0,) * nd, pipeline_mode=pl.Buffered(1))


def _layer_spec(arr, l, block=None):
    block = tuple(arr.shape[1:]) if block is None else block
    nd = len(block)
    return pl.BlockSpec((None,) + block, lambda *_: (l,) + (0,) * nd, pipeline_mode=pl.Buffered(1))


def _layer_norm(r, g, b):
    mu = jnp.mean(r, axis=-1, keepdims=True)
    d = r - mu
    var = jnp.mean(d * d, axis=-1, keepdims=True)
    return d * lax.rsqrt(var + LN_EPS) * g + b


def _rms_norm(x, g):
    return x * lax.rsqrt(jnp.mean(x * x, axis=-1, keepdims=True) + RMS_EPS) * g


def _gelu(x):
    return 0.5 * x * (1.0 + jnp.tanh(math.sqrt(2.0 / math.pi) * (x + 0.044715 * (x * x * x))))


def _log_sigmoid(x):
    return -(jnp.maximum(-x, 0.0) + jnp.log1p(jnp.exp(-jnp.abs(x))))


def _shift_rows(x, d):
    rows, cols = x.shape
    groups = rows // 8
    x3 = x.reshape(groups, 8, cols)
    if d % 8 == 0:
        return jnp.concatenate([jnp.zeros((d // 8, 8, cols), x.dtype), x3[:groups - d // 8]], axis=0).reshape(rows, cols)
    sub = lax.broadcasted_iota(jnp.int32, (groups, 8, cols), 1)
    rolled = pltpu.roll(x3, d, axis=1)
    before = jnp.concatenate([jnp.zeros((1, 8, cols), x.dtype), rolled[:groups - 1]], axis=0)
    return jnp.where(sub >= d, rolled, before).reshape(rows, cols)


def _ffn_ln_chains(xs, wg_ref, wu_ref, wd_ref, g_ref, b_ref):
    xbs = [x.astype(BF16) for x in xs]
    gus = [(jnp.dot(xb, wg_ref[...], preferred_element_type=F32), jnp.dot(xb, wu_ref[...], preferred_element_type=F32))
           for xb in xbs]
    ys = [jnp.dot((gate * jax.nn.sigmoid(gate) * up).astype(BF16), wd_ref[...], preferred_element_type=F32)
          for gate, up in gus]
    return [_layer_norm(ALPHA * x + 0.5 * y, g_ref[...], b_ref[...]) for x, y in zip(xs, ys)]


def _chain_rows(n):
    return [slice(c * FFN_ROWS, (c + 1) * FFN_ROWS) for c in range(n // FFN_ROWS)]


def _ffn_ln_kernel(x_ref, wg_ref, wu_ref, wd_ref, g_ref, b_ref, o_ref):
    chains = _chain_rows(x_ref.shape[0])
    outs = _ffn_ln_chains([x_ref[rows, :] for rows in chains], wg_ref, wu_ref, wd_ref, g_ref, b_ref)
    for rows, out in zip(chains, outs):
        o_ref[rows, :] = out


def _ffn_ln(x2d, wg, wu, wd, g, b, l, tm):
    n = x2d.shape[0]
    row = pl.BlockSpec((tm, D_MODEL), lambda i: (i, 0))
    return pl.pallas_call(
        _ffn_ln_kernel,
        grid=(n // tm,),
        in_specs=[row] + [_layer_spec(a, l) for a in (wg, wu, wd, g, b)],
        out_specs=row,
        out_shape=jax.ShapeDtypeStruct((n, D_MODEL), F32),
        compiler_params=_cparams("parallel"),
        name="ffn_ln",
    )(x2d, wg, wu, wd, g, b)


def _rglru_gates(ax, cw_ref, cb_ref, wg_ref, bg_ref, prev8_sc):
    tm = ax.shape[0]
    groups = tm // 8
    sub = lax.broadcasted_iota(jnp.int32, (groups, 8, D_A), 1)
    ax3 = ax.reshape(groups, 8, D_A)
    prev8 = prev8_sc[...]
    xa = cb_ref[...] + cw_ref[CONV_WIDTH - 1:CONV_WIDTH, :] * ax3
    for k in range(1, CONV_WIDTH):
        rolled = pltpu.roll(ax3, k, axis=1)
        before = jnp.concatenate([pltpu.roll(prev8, k, axis=0)[None], rolled[:groups - 1]], axis=0)
        xa = xa + cw_ref[CONV_WIDTH - 1 - k:CONV_WIDTH - k, :] * jnp.where(sub >= k, rolled, before)
    prev8_sc[...] = ax3[groups - 1]
    xa = xa.reshape(tm, D_A)
    return xa, jnp.dot(xa.astype(BF16), wg_ref[...], preferred_element_type=F32) + bg_ref[...]


def _rglru_scan(xa, gates, ag, sp_ref, gn_ref, hcarry_sc, hs_sc):
    tm = xa.shape[0]
    groups = tm // 8
    grouped = lambda v: v.reshape(groups, 8, D_A)
    sub = lax.broadcasted_iota(jnp.int32, (groups, 8, D_A), 1)
    r = jax.nn.sigmoid(gates[:, :D_A])
    i = jax.nn.sigmoid(gates[:, D_A:])
    log_a = (-RG_C) * r * sp_ref[...]
    a = jnp.exp(log_a)
    y = -jnp.tanh(log_a) * (1.0 + a * a)
    b = jnp.where(y > 0.0, y * lax.rsqrt(y), 0.0) * (i * xa)

    a3, b3 = grouped(a), grouped(b)
    for d in (1, 2, 4):
        valid = sub >= d
        b3 = jnp.where(valid, a3 * pltpu.roll(b3, d, axis=1) + b3, b3)
        a3 = jnp.where(valid, a3 * pltpu.roll(a3, d, axis=1), a3)
    last = hcarry_sc[...]
    for g in range(groups):
        hg = b3[g] + a3[g] * last
        hs_sc[8 * g:8 * g + 8, :] = hg
        last = hg[7:8, :]
    hcarry_sc[...] = last
    return _rms_norm(_gelu(ag) * hs_sc[...], gn_ref[...])


def _in_proj_kernel(x_ref, wa_ref, wb_ref, bf_ref, fmask_ref, ones_ref, hsel_ref, cw_ref, cb_ref, wg_ref, bg_ref, sp_ref,
                    gn_ref, oa_ref, qa_ref, ka_ref, v_ref, cu_ref, base_ref, skip_ref,
                    carry_sc, prev8_sc, hcarry_sc, hs_sc, cu_sc, hist_sc):
    tm = x_ref.shape[0]
    step = pl.program_id(1)

    @pl.when(step == 0)
    def _():
        carry_sc[...] = jnp.zeros_like(carry_sc)
        prev8_sc[...] = jnp.zeros_like(prev8_sc)
        hcarry_sc[...] = jnp.zeros_like(hcarry_sc)
        hist_sc[...] = jnp.zeros_like(hist_sc)

    xb = x_ref[...].astype(BF16)
    o = 2 * D_A
    v_rest = W_SPLIT - (o + 2 * D_B)
    za = jnp.dot(xb, wa_ref[:, 0:o], preferred_element_type=F32)
    zb = jnp.dot(xb, wb_ref[...], preferred_element_type=F32)
    xa, gates = _rglru_gates(za[:, 0:D_A], cw_ref, cb_ref, wg_ref, bg_ref, prev8_sc)

    cu = zb[:, D_B - v_rest:D_B - v_rest + D_C]
    cu_sc[0] = cu[:, :128]
    cu_sc[1] = cu[:, 128:]
    cu_ref[...] = jnp.concatenate([cu_sc[j, pl.ds(t, tm // S5_CHUNK, stride=S5_CHUNK), :]
                                   for t in range(S5_CHUNK) for j in range(2)], axis=1)

    lf = _log_sigmoid(zb[:, W_TAIL - 128:] + bf_ref[...]) * fmask_ref[...] * LOG2E
    lane = lax.broadcasted_iota(jnp.int32, (tm, 128), 1)
    c = lf
    d = 1
    while d < tm:
        c = c + _shift_rows(c, d)
        d *= 2
    carry = carry_sc[...]
    rows = [jnp.broadcast_to(carry[:, F_LANE0 + h:F_LANE0 + h + 1], (1, 128)) for h in range(B_HEADS)]
    pad = jnp.zeros((6, 128), F32)
    base_ref[0, 0] = jnp.stack([jnp.concatenate([rows[2 * p], rows[2 * p + 1], pad], axis=0)
                                for p in range(HEAD_PAIRS)])
    carry_sc[...] = carry + c[tm - 1:tm, :]
    hi = c.astype(BF16).astype(F32)
    r1 = c - hi
    mid = r1.astype(BF16).astype(F32)
    lo = (r1 - mid).astype(BF16).astype(F32)
    pieces = jnp.where(lane < F_LANE0 + 8, hi, jnp.where(lane < F_LANE0 + 16, mid, lo))

    zq = jnp.dot(xb, wa_ref[:, o:W_SPLIT], preferred_element_type=F32)
    oa_ref[...] = _rglru_scan(xa, gates, za[:, D_A:], sp_ref, gn_ref, hcarry_sc, hs_sc).astype(BF16)

    v_ref[...] = jnp.concatenate([zq[:, 2 * D_B:], zb[:, :D_B - v_rest]], axis=1).astype(BF16)
    sqs = []
    for p in range(HEAD_PAIRS):
        qt = zq[:, 128 * p:128 * (p + 1)]
        kt = zq[:, D_B + 128 * p:D_B + 128 * (p + 1)]
        sqs += [qt * qt, kt * kt]
        for e in range(2):
            h = 2 * p + e
            qh = pltpu.roll(qt, B_HEAD_DIM, axis=1) if e else qt
            kh = pltpu.roll(kt, B_HEAD_DIM, axis=1) if e else kt
            qa_ref[0, h] = jnp.where(lane < B_HEAD_DIM, qh * QSCALE, ones_ref[h:h + 1, :]).astype(BF16)
            ka_ref[0, h] = jnp.where(lane < B_HEAD_DIM, kh, -pieces).astype(BF16)
    norms = jnp.dot(jnp.concatenate(sqs, axis=1).astype(BF16), hsel_ref[...], preferred_element_type=F32)
    top = jnp.max(norms, axis=0, keepdims=True)
    qn2 = top
    kn2 = pltpu.roll(top, 120, axis=1)

    qn = jnp.sqrt(qn2) * (QSCALE * NORM_GUARD)
    kn_here = jnp.sqrt(kn2)
    alive = jnp.ones((1, 128), F32)
    count = jnp.zeros((1, 128), F32)
    for kb in range(hist_sc.shape[0]):
        bound = qn * (jnp.sqrt(hist_sc[kb, 1:2, :]) + kn_here) + (carry - hist_sc[kb, 0:1, :])
        ok = jnp.where(bound <= SKIP_BELOW, 1.0, 0.0)
        ok = ok * pltpu.roll(ok, 127, axis=1)
        alive = alive * jnp.where(kb < step, ok, 0.0)
        count = count + alive
    skip_ref[0, 0] = jnp.broadcast_to(count, (8, 128))
    hist_sc[step, 0:1, :] = carry + c[tm - 1:tm, :]
    hist_sc[step, 1:2, :] = kn2


def _in_proj(x2d, wa, wb, bf_row, fmask, ones, hsel, cw, cb, wg, bg, sp, gn, l, bsz, seq, tm):
    ns = seq // tm
    n = bsz * seq
    tok = lambda rows, width: pl.BlockSpec((rows, width), lambda b, s: (b * ns + s, 0))
    heads = pl.BlockSpec((1, B_HEADS, tm, 128), lambda b, s: (b, 0, s, 0))
    return pl.pallas_call(
        _in_proj_kernel,
        grid=(bsz, ns),
        in_specs=[tok(tm, D_MODEL), _layer_spec(wa, l), _layer_spec(wb, l),
                  _layer_spec(bf_row, l), _const_spec((1, 128)), _const_spec((B_HEADS, 128)), _const_spec(hsel.shape)]
                 + [_layer_spec(a, l) for a in (cw, cb, wg, bg, sp, gn)],
        out_specs=[tok(tm, D_A), heads, heads, tok(tm, D_B), tok(tm // S5_CHUNK, S5_CHUNK * D_C),
                   pl.BlockSpec((1, 1, HEAD_PAIRS, 8, 128), lambda b, s: (b, s, 0, 0, 0)),
                   pl.BlockSpec((1, 1, 8, 128), lambda b, s: (b, s, 0, 0))],
        out_shape=[jax.ShapeDtypeStruct((n, D_A), BF16),
                   jax.ShapeDtypeStruct((bsz, B_HEADS, seq, 128), BF16),
                   jax.ShapeDtypeStruct((bsz, B_HEADS, seq, 128), BF16),
                   jax.ShapeDtypeStruct((n, D_B), BF16),
                   jax.ShapeDtypeStruct((n // S5_CHUNK, S5_CHUNK * D_C), F32),
                   jax.ShapeDtypeStruct((bsz, ns, HEAD_PAIRS, 8, 128), F32),
                   jax.ShapeDtypeStruct((bsz, ns, 8, 128), F32)],
        scratch_shapes=[pltpu.VMEM((1, 128), F32), pltpu.VMEM((8, D_A), F32), pltpu.VMEM((1, D_A), F32),
                        pltpu.VMEM((tm, D_A), F32), pltpu.VMEM((2, tm, 128), F32), pltpu.VMEM((ns, 8, 128), F32)],
        compiler_params=_cparams("parallel", "arbitrary"),
        name="in_proj",
    )(x2d, wa, wb, bf_row, fmask, ones, hsel, cw, cb, wg, bg, sp, gn)


def _fox_kernel(skip_ref, qa_ref, ka_ref, v_ref, base_ref, o_ref, vaug_sc, acc_sc, m_sc, *, tq):
    seq = v_ref.shape[1]
    lane = lax.broadcasted_iota(jnp.int32, (ROWS, 128), 1)
    sub = lax.broadcasted_iota(jnp.int32, (ROWS, 128), 0)

    lane_v = lax.broadcasted_iota(jnp.int32, (tq, 128), 1)
    for c in range(seq // tq):
        vc = v_ref[0, c * tq:(c + 1) * tq, :].astype(F32)
        vaug_sc[0, c * tq:(c + 1) * tq, :] = jnp.concatenate(
            [jnp.where(lane_v < B_HEAD_DIM, vc, 0.0), jnp.where(lane_v == 0, 1.0, 0.0)], axis=1).astype(BF16)
        vaug_sc[1, c * tq:(c + 1) * tq, :] = jnp.concatenate(
            [jnp.where(lane_v >= B_HEAD_DIM, vc, 0.0), jnp.where(lane_v == 1, 1.0, 0.0)], axis=1).astype(BF16)

    def query_block(qi, carry):
        first = skip_ref[pl.program_id(0), pl.program_id(1), qi]
        past = qi - first
        q0 = pl.multiple_of(qi * tq, tq)
        acc_sc[...] = jnp.zeros_like(acc_sc)
        m_sc[...] = jnp.full_like(m_sc, NEG)
        base_q = base_ref[0, qi, 0]

        def tasks_of(k0, base_k, diag):
            deltas = [base_q[j:j + 1, :] - base_k[j:j + 1, :] for j in range(2)]
            return [(k0, deltas, diag, r) for r in range(tq // ROWS)]

        def qk(task):
            k0, _, diag, r = task
            n = (r + 1) * ROWS if diag else tq
            rows = pl.ds(pl.multiple_of(q0 + r * ROWS, ROWS), ROWS)
            return [lax.dot_general(qa_ref[0, j, rows, :], ka_ref[0, j, pl.ds(k0, n), :],
                                    (((1,), (1,)), ((), ())), preferred_element_type=F32) for j in range(2)]

        def finish(task, s_pair):
            k0, deltas, diag, r = task
            rows = slice(r * ROWS, (r + 1) * ROWS)
            n = (r + 1) * ROWS if diag else tq
            ps, alphas = [], []
            for j in range(2):
                s = s_pair[j]
                tiles = [s[:, c * 128:(c + 1) * 128] for c in range(n // 128)]
                if diag:
                    tiles[-1] = jnp.where(lane <= sub, tiles[-1], NEG)
                tmax = tiles[0]
                for t in tiles[1:]:
                    tmax = jnp.maximum(tmax, t)
                m_cur = jnp.max(tmax, axis=-1, keepdims=True) + deltas[j]
                m_prev = m_sc[j, rows, :]
                m_new = jnp.maximum(m_prev, m_cur)
                alphas.append(jnp.exp2(m_prev - m_new))
                m_sc[j, rows, :] = m_new
                mt = m_new - deltas[j]
                ps.append(jnp.concatenate([jnp.exp2(t - mt) for t in tiles], axis=1).astype(BF16))
            vk = jnp.concatenate([vaug_sc[0, pl.ds(k0, n), :], vaug_sc[1, pl.ds(k0, n), :]], axis=0)
            pv = jnp.dot(jnp.concatenate(ps, axis=1), vk, preferred_element_type=F32)
            a = jnp.concatenate([jnp.where(lane < B_HEAD_DIM, alphas[0], alphas[1]),
                                 jnp.where(lane == 0, alphas[0], alphas[1])], axis=1)
            acc_sc[rows, :] = a * acc_sc[rows, :] + pv

        def run(tasks):
            s_next = qk(tasks[0])
            for i, task in enumerate(tasks):
                s_cur = s_next
                if i + 1 < len(tasks):
                    s_next = qk(tasks[i + 1])
                finish(task, s_cur)

        def past_tasks(ki):
            return tasks_of(pl.multiple_of(ki * tq, tq), base_ref[0, ki, 0], False)

        def quad(i, c):
            run([t for u in range(4) for t in past_tasks(first + 4 * i + u)])
            return c

        lax.fori_loop(0, past // 4, quad, 0)
        rem0 = first + (past // 4) * 4

        for tail in range(4):
            @pl.when(past % 4 == tail)
            def _(tail=tail):
                run([t for u in range(tail) for t in past_tasks(rem0 + u)] + tasks_of(q0, base_q, True))
                acc = acc_sc[...]
                lane_o = lax.broadcasted_iota(jnp.int32, (tq, 128), 1)
                o_ref[0, pl.ds(q0, tq), :] = acc[:, :128] / jnp.where(lane_o < B_HEAD_DIM, acc[:, 128:129], acc[:, 129:130])

        return carry

    lax.fori_loop(0, seq // tq, query_block, 0)


def _fox(skip, qa, ka, v3, base, tq):
    bsz, seq, _ = v3.shape
    ns = seq // tq
    heads = pl.BlockSpec((1, 2, seq, 128), lambda b, h, sk: (b, h, 0, 0))
    return pl.pallas_call(
        functools.partial(_fox_kernel, tq=tq),
        grid_spec=pltpu.PrefetchScalarGridSpec(
            num_scalar_prefetch=1,
            grid=(bsz, HEAD_PAIRS),
            in_specs=[heads, heads,
                      pl.BlockSpec((1, seq, 128), lambda b, h, sk: (b, 0, h)),
                      pl.BlockSpec((1, ns, 1, 8, 128), lambda b, h, sk: (b, 0, h, 0, 0))],
            out_specs=pl.BlockSpec((1, seq, 128), lambda b, h, sk: (b, 0, h)),
            scratch_shapes=[pltpu.VMEM((2, seq, 256), BF16), pltpu.VMEM((tq, 256), F32),
                            pltpu.VMEM((2, tq, 128), F32)]),
        out_shape=jax.ShapeDtypeStruct((bsz, seq, D_B), F32),
        compiler_params=_cparams("parallel", "parallel"),
        name="fox_attention",
    )(skip, qa, ka, v3, base)


def _s5_kernel(u_ref, m_ref, bm_ref, cm_ref, lev_ref, pw_ref, d_ref, wglu_ref, gn_ref, o_ref, hs_sc):
    u = u_ref[...]
    rows = u.shape[0]
    groups = rows // 8
    ub = u.astype(BF16)
    sub8 = lax.broadcasted_iota(jnp.int32, (8, S5_HSTATE), 0)
    y_halves = []
    for h in range(2):
        re_cols = slice((2 * h) * S5_HSTATE, (2 * h + 1) * S5_HSTATE)
        im_cols = slice((2 * h + 1) * S5_HSTATE, (2 * h + 2) * S5_HSTATE)
        uh = jnp.concatenate([ub[:, t * D_C + h * 128: t * D_C + (h + 1) * 128] for t in range(S5_CHUNK)],
                             axis=1)
        y = jnp.dot(uh, m_ref[h], preferred_element_type=F32)
        st = jnp.dot(uh, bm_ref[h], preferred_element_type=F32)
        re3 = st[:, :S5_HSTATE].reshape(groups, 8, S5_HSTATE)
        im3 = st[:, S5_HSTATE:].reshape(groups, 8, S5_HSTATE)
        for lv, d in enumerate((1, 2, 4)):
            ar = jnp.where(sub8 >= d, lev_ref[lv:lv + 1, re_cols], 0.0)
            ai = jnp.where(sub8 >= d, lev_ref[lv:lv + 1, im_cols], 0.0)
            rr = pltpu.roll(re3, d, axis=1)
            ri = pltpu.roll(im3, d, axis=1)
            re3, im3 = re3 + (ar * rr - ai * ri), im3 + (ar * ri + ai * rr)
        pr, pi = pw_ref[:, re_cols], pw_ref[:, im_cols]
        cre = jnp.zeros((1, S5_HSTATE), F32)
        cim = jnp.zeros((1, S5_HSTATE), F32)
        for g in range(groups):
            hr = re3[g] + (pr * cre - pi * cim)
            hi = im3[g] + (pr * cim + pi * cre)
            hs_sc[0, 8 * g:8 * g + 8, :] = hr
            hs_sc[1, 8 * g:8 * g + 8, :] = hi
            cre, cim = hr[7:8, :], hi[7:8, :]
        hprev = jnp.concatenate([_shift_rows(hs_sc[0], 1), _shift_rows(hs_sc[1], 1)], axis=1)
        y = y + jnp.dot(hprev.astype(BF16), cm_ref[h], preferred_element_type=F32)
        y_halves.append(y)
    outs = []
    for t in range(S5_CHUNK):
        yt = jnp.concatenate([y_halves[0][:, t * 128:(t + 1) * 128], y_halves[1][:, t * 128:(t + 1) * 128]], axis=1)
        yt = _gelu(yt + d_ref[...] * u[:, t * D_C:(t + 1) * D_C])
        gl = jnp.dot(yt.astype(BF16), wglu_ref[...], preferred_element_type=F32)
        outs.append(_rms_norm(yt * jax.nn.sigmoid(gl), gn_ref[...]))
    o_ref[...] = jnp.concatenate(outs, axis=1)


def _s5(u2, m, bm, cm, lev, pw, d, wglu, gn, l, rows_per_seq):
    n, width = u2.shape
    blk = pl.BlockSpec((rows_per_seq, width), lambda b: (b, 0))
    return pl.pallas_call(
        _s5_kernel,
        grid=(n // rows_per_seq,),
        in_specs=[blk] + [_layer_spec(a, l) for a in (m, bm, cm, lev, pw, d, wglu, gn)],
        out_specs=blk,
        out_shape=jax.ShapeDtypeStruct(u2.shape, F32),
        scratch_shapes=[pltpu.VMEM((2, rows_per_seq, S5_HSTATE), F32)],
        compiler_params=_cparams("parallel"),
        name="s5",
    )(u2, m, bm, cm, lev, pw, d, wglu, gn)


def _s5_prepare(a_re, a_im, log_dt, b_re, b_im, c_re, c_im):
    hi = lax.Precision.HIGHEST
    T = S5_CHUNK
    nl = a_re.shape[0]
    dt = jnp.exp(log_dt)[:, :, None]
    lr, li = a_re * dt, a_im * dt

    def cpow(tau):
        mag = jnp.exp(lr[:, None] * tau[None, :, None, None])
        ang = li[:, None] * tau[None, :, None, None]
        return mag * jnp.cos(ang), mag * jnp.sin(ang)

    def halves(z):
        return jnp.moveaxis(z.reshape(z.shape[:2] + (2, S5_HALF) + z.shape[3:]), 2, 1)

    ar1, ai1 = cpow(jnp.ones((1,), F32))
    nr, ni = ar1[:, 0] - 1.0, ai1[:, 0]
    den = a_re * a_re + a_im * a_im
    fr, fi = (nr * a_re + ni * a_im) / den, (ni * a_re - nr * a_im) / den
    bbr = fr[..., None] * b_re - fi[..., None] * b_im
    bbi = fr[..., None] * b_im + fi[..., None] * b_re

    dr, di = cpow((T - 1) - jnp.arange(T, dtype=F32))
    abr = dr[..., None] * bbr[:, None] - di[..., None] * bbi[:, None]
    abi = dr[..., None] * bbi[:, None] + di[..., None] * bbr[:, None]

    def same_group(rows, rdiv, cols, cdiv, cmod=S5_HALF):
        r = (jnp.arange(rows) // rdiv)[:, None]
        c = ((jnp.arange(cols) // cdiv) % cmod)[None, :]
        return (r == c).astype(F32)

    def tile_eye(n, reps):
        return (jnp.arange(n)[:, None] == (jnp.arange(n * reps) % n)[None, :]).astype(F32)

    kern = (jnp.einsum('lgop,ljgpi->ljgio', c_re, abr, precision=hi)
            - jnp.einsum('lgop,ljgpi->ljgio', c_im, abi, precision=hi))
    kd = jnp.einsum('lhjrc,cn->lhjrn', halves(kern).reshape(nl, 2, T, 128, C_GROUP), tile_eye(C_GROUP, S5_HALF))
    kd = kd * same_group(128, C_GROUP, 128, C_GROUP)
    ext = jnp.concatenate([kd, jnp.zeros((nl, 2, T - 1, 128, 128), F32)], axis=2)
    m = jnp.concatenate([ext[:, :, T - 1 - b:2 * T - 1 - b] for b in range(T)], axis=-1)
    m = m.reshape(nl, 2, S5_HCOLS, S5_HCOLS)

    def state_in(ab):
        y = jnp.swapaxes(halves(ab), -1, -2).reshape(nl, 2, T, 128, C_STATE)
        return jnp.einsum('lhjrp,pn->lhjrn', y, tile_eye(C_STATE, S5_HALF)) * same_group(128, C_GROUP, S5_HSTATE, C_STATE)

    bm = jnp.concatenate([state_in(abr), state_in(abi)], axis=-1).reshape(nl, 2, S5_HCOLS, 2 * S5_HSTATE)

    ur, ui = cpow(jnp.arange(1, T + 1, dtype=F32))
    so_r = c_re[:, None] * ur[:, :, :, None, :] - c_im[:, None] * ui[:, :, :, None, :]
    so_i = c_re[:, None] * ui[:, :, :, None, :] + c_im[:, None] * ur[:, :, :, None, :]
    expand = (jnp.arange(T * C_GROUP)[:, None] // C_GROUP == jnp.arange(S5_HCOLS)[None, :] // 128) & (
        jnp.arange(T * C_GROUP)[:, None] % C_GROUP == jnp.arange(S5_HCOLS)[None, :] % C_GROUP)

    def state_out(so):
        y = jnp.transpose(halves(so), (0, 1, 3, 5, 2, 4)).reshape(nl, 2, S5_HSTATE, T * C_GROUP)
        return jnp.einsum('lhrc,cn->lhrn', y, expand.astype(F32)) * same_group(S5_HSTATE, C_STATE, S5_HCOLS, C_GROUP)

    cm = jnp.concatenate([state_out(so_r), state_out(-so_i)], axis=2)

    def state_layout(zr, zi):
        n = zr.shape[1]
        parts = []
        for h in range(2):
            gs = slice(h * S5_HALF, (h + 1) * S5_HALF)
            parts += [zr[:, :, gs].reshape(nl, n, S5_HSTATE), zi[:, :, gs].reshape(nl, n, S5_HSTATE)]
        return jnp.concatenate(parts, axis=-1)

    lev = state_layout(*cpow(jnp.asarray([T, 2 * T, 4 * T], F32)))
    pw = state_layout(*cpow(T * jnp.arange(1, 9, dtype=F32)))
    return m.astype(BF16), bm.astype(BF16), cm.astype(BF16), lev, pw


def _mix_ffn_kernel(oa_ref, ob_ref, oc_ref, x_ref, gb_ref, wo_ref, g2_ref, b2_ref, wg_ref, wu_ref, wd_ref,
                    g3_ref, b3_ref, o_ref, oc_sc):
    tm = x_ref.shape[0]
    for t in range(S5_CHUNK):
        for j in range(2):
            oc_sc[j, pl.ds(t, tm // S5_CHUNK, stride=S5_CHUNK), :] = oc_ref[:, t * D_C + j * 128:t * D_C + (j + 1) * 128]
    chains = _chain_rows(tm)
    mixes = []
    for rows in chains:
        nb = _rms_norm(ob_ref[rows, :], gb_ref[...]).astype(BF16)
        o = jnp.concatenate([oa_ref[rows, :], nb, oc_sc[0, rows, :].astype(BF16), oc_sc[1, rows, :].astype(BF16)], axis=1)
        mixes.append(jnp.dot(o, wo_ref[...], preferred_element_type=F32))
    xs = [_layer_norm(ALPHA * x_ref[rows, :] + mix, g2_ref[...], b2_ref[...]) for rows, mix in zip(chains, mixes)]
    for rows, out in zip(chains, _ffn_ln_chains(xs, wg_ref, wu_ref, wd_ref, g3_ref, b3_ref)):
        o_ref[rows, :] = out


def _mix_ffn(oa2, ob2, oc2, x2d, gb, wo, g2, b2, wg, wu, wd, g3, b3, l, tm):
    tok = lambda rows, width: pl.BlockSpec((rows, width), lambda i: (i, 0))
    return pl.pallas_call(
        _mix_ffn_kernel,
        grid=(x2d.shape[0] // tm,),
        in_specs=[tok(tm, D_A), tok(tm, D_B), tok(tm // S5_CHUNK, S5_CHUNK * D_C), tok(tm, D_MODEL)]
                 + [_layer_spec(a, l) for a in (gb, wo, g2, b2, wg, wu, wd, g3, b3)],
        out_specs=tok(tm, D_MODEL),
        out_shape=jax.ShapeDtypeStruct(x2d.shape, F32),
        scratch_shapes=[pltpu.VMEM((2, tm, 128), F32)],
        compiler_params=_cparams("parallel"),
        name="mix_ffn_ln",
    )(oa2, ob2, oc2, x2d, gb, wo, g2, b2, wg, wu, wd, g3, b3)


def _block_diag(w):
    eye = jnp.eye(A_BLOCKS, dtype=w.dtype)
    return jnp.einsum('lhij,hk->lhikj', w, eye).reshape(w.shape[0], D_A, D_A)


def _pick(n, pref):
    t = min(n, pref)
    while n % t:
        t //= 2
    return t


def kernel(x, ffn1_w_gate, ffn1_w_up, ffn1_w_down, ln1_g, ln1_b, w_in, conv_w, conv_b, rg_w_a, rg_b_a, rg_w_x, rg_b_x, rg_lambda, fox_b_f, s5_a_re, s5_a_im, s5_log_dt, s5_b_re, s5_b_im, s5_c_re, s5_c_im, s5_d, s5_w_glu, mix_norm_g, w_out, ln2_g, ln2_b, ffn2_w_gate, ffn2_w_up, ffn2_w_down, ln3_g, ln3_b):
    bsz, seq, _ = x.shape
    n = bsz * seq
    tm = _pick(seq, 512)
    vec = lambda v: v[:, None, :]
    bf = lambda w: w.astype(BF16)

    f1, f2 = [bf(w) for w in (ffn1_w_gate, ffn1_w_up, ffn1_w_down)], [bf(w) for w in (ffn2_w_gate, ffn2_w_up, ffn2_w_down)]
    cuts = np.cumsum([D_A, D_A, D_B, D_B, D_B, B_HEADS])
    wa = bf(w_in[:, :, :W_SPLIT])
    wt = bf(w_in[:, :, W_SPLIT:])
    place = np.zeros((B_HEADS, 128), np.float32)
    for piece in range(3):
        place[np.arange(B_HEADS), F_LANE0 + 8 * piece + np.arange(B_HEADS)] = 1.0
    place = jnp.asarray(place)
    hsel = np.zeros((2 * HEAD_PAIRS * 128, 128), np.float32)
    for p in range(HEAD_PAIRS):
        for is_k in range(2):
            for e in range(2):
                r0 = (2 * p + is_k) * 128 + e * B_HEAD_DIM
                hsel[r0:r0 + B_HEAD_DIM, F_LANE0 + 8 * is_k + 2 * p + e] = 1.0
    hsel = jnp.asarray(hsel, BF16)
    fcols = jnp.einsum('ldh,hn->ldn', wt[:, :, cuts[4] - W_SPLIT:cuts[5] - W_SPLIT], place.astype(BF16))
    wb = jnp.concatenate([wt[:, :, :cuts[4] - W_SPLIT], wt[:, :, cuts[5] - W_SPLIT:], fcols], axis=-1)
    bf_row = jnp.einsum('lh,hn->ln', fox_b_f, place, precision=lax.Precision.HIGHEST)[:, None, :]
    fmask = jnp.max(place, axis=0, keepdims=True)
    wgates = bf(jnp.concatenate([_block_diag(rg_w_a), _block_diag(rg_w_x)], axis=-1))
    bgates = jnp.concatenate([rg_b_a, rg_b_x], axis=-1)[:, None, :]
    g_a, g_b, g_c = (vec(g) for g in jnp.split(mix_norm_g, [D_A, D_A + D_B], axis=-1))
    sp = vec(jax.nn.softplus(-rg_lambda))
    m, bm, cm, lev, pw = _s5_prepare(s5_a_re, s5_a_im, s5_log_dt, s5_b_re, s5_b_im, s5_c_re, s5_c_im)
    wglu, wo = bf(s5_w_glu), bf(w_out)

    h = x.reshape(n, D_MODEL)
    for l in range(DEPTH):
        h = _ffn_ln(h, f1[0], f1[1], f1[2], vec(ln1_g), vec(ln1_b), l, tm)
        oa, qa, ka, v, cu, base, skip = _in_proj(h, wa, wb, bf_row, fmask, place, hsel, conv_w, vec(conv_b), wgates,
                                                 bgates, sp, g_a, l, bsz, seq, tm)
        skip = jnp.transpose(skip[:, :, 0, F_LANE0:F_LANE0 + B_HEADS:2], (0, 2, 1)).astype(jnp.int32)
        ob = _fox(skip, qa, ka, v.reshape(bsz, seq, D_B), base, tm)
        oc = _s5(cu, m, bm, cm, lev, pw, vec(s5_d), wglu, g_c, l, seq // S5_CHUNK)
        h = _mix_ffn(oa, ob.reshape(n, D_B), oc, h, g_b, wo, vec(ln2_g), vec(ln2_b), f2[0], f2[1], f2[2],
                     vec(ln3_g), vec(ln3_b), l, tm)
    return h.reshape(bsz, seq, D_MODEL)
```

```python
import functools
import math

import numpy as np
import jax
import jax.numpy as jnp
from jax import lax
from jax.experimental import pallas as pl
from jax.experimental.pallas import tpu as pltpu

F32 = jnp.float32
BF16 = jnp.bfloat16

D_MODEL = 1024
D_FF = 2816
D_A = 384
A_BLOCKS = 6
A_BLOCK_DIM = 64
CONV_WIDTH = 4
RG_C = 8.0
D_B = 384
B_HEADS = 6
B_HEAD_DIM = 64
HEAD_PAIRS = B_HEADS // 2
D_C = 256
C_GROUP = 16
C_GROUPS = 16
C_STATE = 64
W_SPLIT = 1792
W_TAIL = 2 * D_A + 3 * D_B - W_SPLIT + D_C + 128
F_LANE0 = 64
LOG2E = math.log2(math.e)
QSCALE = LOG2E * B_HEAD_DIM ** -0.5
NEG = -1e30
SKIP_BELOW = -160.0
NORM_GUARD = 1.0 + 2.0 ** -7
ROWS = 128
FFN_ROWS = 256
DEPTH = 2
ALPHA = (2 * DEPTH) ** 0.25
LN_EPS = 1e-5
RMS_EPS = 1e-6

S5_CHUNK = 8
S5_HALF = C_GROUPS // 2
S5_HCOLS = S5_CHUNK * S5_HALF * C_GROUP
S5_HSTATE = S5_HALF * C_STATE

VMEM_LIMIT = 56 * 1024 * 1024


def _cparams(*sem):
    return pltpu.CompilerParams(dimension_semantics=sem, vmem_limit_bytes=VMEM_LIMIT)


def _const_spec(shape):
    nd = len(shape)
    return pl.BlockSpec(shape, lambda *_: (0,) * nd, pipeline_mode=pl.Buffered(1))


def _layer_spec(arr, l, block=None):
    block = tuple(arr.shape[1:]) if block is None else block
    nd = len(block)
    return pl.BlockSpec((None,) + block, lambda *_: (l,) + (0,) * nd, pipeline_mode=pl.Buffered(1))


def _layer_norm(r, g, b):
    mu = jnp.mean(r, axis=-1, keepdims=True)
    d = r - mu
    var = jnp.mean(d * d, axis=-1, keepdims=True)
    return d * lax.rsqrt(var + LN_EPS) * g + b


def _rms_norm(x, g):
    return x * lax.rsqrt(jnp.mean(x * x, axis=-1, keepdims=True) + RMS_EPS) * g


def _gelu(x):
    return 0.5 * x * (1.0 + jnp.tanh(math.sqrt(2.0 / math.pi) * (x + 0.044715 * (x * x * x))))


def _log_sigmoid(x):
    return -(jnp.maximum(-x, 0.0) + jnp.log1p(jnp.exp(-jnp.abs(x))))


def _shift_rows(x, d):
    rows, cols = x.shape
    groups = rows // 8
    x3 = x.reshape(groups, 8, cols)
    if d % 8 == 0:
        return jnp.concatenate([jnp.zeros((d // 8, 8, cols), x.dtype), x3[:groups - d // 8]], axis=0).reshape(rows, cols)
    sub = lax.broadcasted_iota(jnp.int32, (groups, 8, cols), 1)
    rolled = pltpu.roll(x3, d, axis=1)
    before = jnp.concatenate([jnp.zeros((1, 8, cols), x.dtype), rolled[:groups - 1]], axis=0)
    return jnp.where(sub >= d, rolled, before).reshape(rows, cols)


def _ffn_ln_chains(xs, wg_ref, wu_ref, wd_ref, g_ref, b_ref):
    xbs = [x.astype(BF16) for x in xs]
    gus = [(jnp.dot(xb, wg_ref[...], preferred_element_type=F32), jnp.dot(xb, wu_ref[...], preferred_element_type=F32))
           for xb in xbs]
    ys = [jnp.dot((gate * jax.nn.sigmoid(gate) * up).astype(BF16), wd_ref[...], preferred_element_type=F32)
          for gate, up in gus]
    return [_layer_norm(ALPHA * x + 0.5 * y, g_ref[...], b_ref[...]) for x, y in zip(xs, ys)]


def _chain_rows(n):
    return [slice(c * FFN_ROWS, (c + 1) * FFN_ROWS) for c in range(n // FFN_ROWS)]


def _ffn_ln_kernel(x_ref, wg_ref, wu_ref, wd_ref, g_ref, b_ref, o_ref):
    chains = _chain_rows(x_ref.shape[0])
    outs = _ffn_ln_chains([x_ref[rows, :] for rows in chains], wg_ref, wu_ref, wd_ref, g_ref, b_ref)
    for rows, out in zip(chains, outs):
        o_ref[rows, :] = out


def _ffn_ln(x2d, wg, wu, wd, g, b, l, tm):
    n = x2d.shape[0]
    row = pl.BlockSpec((tm, D_MODEL), lambda i: (i, 0))
    return pl.pallas_call(
        _ffn_ln_kernel,
        grid=(n // tm,),
        in_specs=[row] + [_layer_spec(a, l) for a in (wg, wu, wd, g, b)],
        out_specs=row,
        out_shape=jax.ShapeDtypeStruct((n, D_MODEL), F32),
        compiler_params=_cparams("parallel"),
        name="ffn_ln",
    )(x2d, wg, wu, wd, g, b)


def _rglru_gates(ax, cw_ref, cb_ref, wg_ref, bg_ref, prev8_sc):
    tm = ax.shape[0]
    groups = tm // 8
    sub = lax.broadcasted_iota(jnp.int32, (groups, 8, D_A), 1)
    ax3 = ax.reshape(groups, 8, D_A)
    prev8 = prev8_sc[...]
    xa = cb_ref[...] + cw_ref[CONV_WIDTH - 1:CONV_WIDTH, :] * ax3
    for k in range(1, CONV_WIDTH):
        rolled = pltpu.roll(ax3, k, axis=1)
        before = jnp.concatenate([pltpu.roll(prev8, k, axis=0)[None], rolled[:groups - 1]], axis=0)
        xa = xa + cw_ref[CONV_WIDTH - 1 - k:CONV_WIDTH - k, :] * jnp.where(sub >= k, rolled, before)
    prev8_sc[...] = ax3[groups - 1]
    xa = xa.reshape(tm, D_A)
    return xa, jnp.dot(xa.astype(BF16), wg_ref[...], preferred_element_type=F32) + bg_ref[...]


def _rglru_scan(xa, gates, ag, sp_ref, gn_ref, hcarry_sc, hs_sc):
    tm = xa.shape[0]
    groups = tm // 8
    grouped = lambda v: v.reshape(groups, 8, D_A)
    sub = lax.broadcasted_iota(jnp.int32, (groups, 8, D_A), 1)
    r = jax.nn.sigmoid(gates[:, :D_A])
    i = jax.nn.sigmoid(gates[:, D_A:])
    log_a = (-RG_C) * r * sp_ref[...]
    a = jnp.exp(log_a)
    y = -jnp.tanh(log_a) * (1.0 + a * a)
    b = jnp.where(y > 0.0, y * lax.rsqrt(y), 0.0) * (i * xa)

    a3, b3 = grouped(a), grouped(b)
    for d in (1, 2, 4):
        valid = sub >= d
        b3 = jnp.where(valid, a3 * pltpu.roll(b3, d, axis=1) + b3, b3)
        a3 = jnp.where(valid, a3 * pltpu.roll(a3, d, axis=1), a3)
    last = hcarry_sc[...]
    for g in range(groups):
        hg = b3[g] + a3[g] * last
        hs_sc[8 * g:8 * g + 8, :] = hg
        last = hg[7:8, :]
    hcarry_sc[...] = last
    return _rms_norm(_gelu(ag) * hs_sc[...], gn_ref[...])


def _in_proj_kernel(x_ref, wa_ref, wb_ref, bf_ref, fmask_ref, ones_ref, hsel_ref, cw_ref, cb_ref, wg_ref, bg_ref, sp_ref,
                    gn_ref, oa_ref, qa_ref, ka_ref, v_ref, cu_ref, base_ref, skip_ref,
                    carry_sc, prev8_sc, hcarry_sc, hs_sc, cu_sc, hist_sc):
    tm = x_ref.shape[0]
    step = pl.program_id(1)

    @pl.when(step == 0)
    def _():
        carry_sc[...] = jnp.zeros_like(carry_sc)
        prev8_sc[...] = jnp.zeros_like(prev8_sc)
        hcarry_sc[...] = jnp.zeros_like(hcarry_sc)
        hist_sc[...] = jnp.zeros_like(hist_sc)

    xb = x_ref[...].astype(BF16)
    o = 2 * D_A
    v_rest = W_SPLIT - (o + 2 * D_B)
    za = jnp.dot(xb, wa_ref[:, 0:o], preferred_element_type=F32)
    zb = jnp.dot(xb, wb_ref[...], preferred_element_type=F32)
    xa, gates = _rglru_gates(za[:, 0:D_A], cw_ref, cb_ref, wg_ref, bg_ref, prev8_sc)

    cu = zb[:, D_B - v_rest:D_B - v_rest + D_C]
    cu_sc[0] = cu[:, :128]
    cu_sc[1] = cu[:, 128:]
    cu_ref[...] = jnp.concatenate([cu_sc[j, pl.ds(t, tm // S5_CHUNK, stride=S5_CHUNK), :]
                                   for t in range(S5_CHUNK) for j in range(2)], axis=1)

    lf = _log_sigmoid(zb[:, W_TAIL - 128:] + bf_ref[...]) * fmask_ref[...] * LOG2E
    lane = lax.broadcasted_iota(jnp.int32, (tm, 128), 1)
    c = lf
    d = 1
    while d < tm:
        c = c + _shift_rows(c, d)
        d *= 2
    carry = carry_sc[...]
    rows = [jnp.broadcast_to(carry[:, F_LANE0 + h:F_LANE0 + h + 1], (1, 128)) for h in range(B_HEADS)]
    pad = jnp.zeros((6, 128), F32)
    base_ref[0, 0] = jnp.stack([jnp.concatenate([rows[2 * p], rows[2 * p + 1], pad], axis=0)
                                for p in range(HEAD_PAIRS)])
    carry_sc[...] = carry + c[tm - 1:tm, :]
    hi = c.astype(BF16).astype(F32)
    r1 = c - hi
    mid = r1.astype(BF16).astype(F32)
    lo = (r1 - mid).astype(BF16).astype(F32)
    pieces = jnp.where(lane < F_LANE0 + 8, hi, jnp.where(lane < F_LANE0 + 16, mid, lo))

    zq = jnp.dot(xb, wa_ref[:, o:W_SPLIT], preferred_element_type=F32)
    oa_ref[...] = _rglru_scan(xa, gates, za[:, D_A:], sp_ref, gn_ref, hcarry_sc, hs_sc).astype(BF16)

    v_ref[...] = jnp.concatenate([zq[:, 2 * D_B:], zb[:, :D_B - v_rest]], axis=1).astype(BF16)
    sqs = []
    for p in range(HEAD_PAIRS):
        qt = zq[:, 128 * p:128 * (p + 1)]
        kt = zq[:, D_B + 128 * p:D_B + 128 * (p + 1)]
        sqs += [qt * qt, kt * kt]
        for e in range(2):
            h = 2 * p + e
            qh = pltpu.roll(qt, B_HEAD_DIM, axis=1) if e else qt
            kh = pltpu.roll(kt, B_HEAD_DIM, axis=1) if e else kt
            qa_ref[0, h] = jnp.where(lane < B_HEAD_DIM, qh * QSCALE, ones_ref[h:h + 1, :]).astype(BF16)
            ka_ref[0, h] = jnp.where(lane < B_HEAD_DIM, kh, -pieces).astype(BF16)
    norms = jnp.dot(jnp.concatenate(sqs, axis=1).astype(BF16), hsel_ref[...], preferred_element_type=F32)
    top = jnp.max(norms, axis=0, keepdims=True)
    qn2 = top
    kn2 = pltpu.roll(top, 120, axis=1)

    qn = jnp.sqrt(qn2) * (QSCALE * NORM_GUARD)
    kn_here = jnp.sqrt(kn2)
    alive = jnp.ones((1, 128), F32)
    count = jnp.zeros((1, 128), F32)
    for kb in range(hist_sc.shape[0]):
        bound = qn * (jnp.sqrt(hist_sc[kb, 1:2, :]) + kn_here) + (carry - hist_sc[kb, 0:1, :])
        ok = jnp.where(bound <= SKIP_BELOW, 1.0, 0.0)
        ok = ok * pltpu.roll(ok, 127, axis=1)
        alive = alive * jnp.where(kb < step, ok, 0.0)
        count = count + alive
    skip_ref[0, 0] = jnp.broadcast_to(count, (8, 128))
    hist_sc[step, 0:1, :] = carry + c[tm - 1:tm, :]
    hist_sc[step, 1:2, :] = kn2


def _in_proj(x2d, wa, wb, bf_row, fmask, ones, hsel, cw, cb, wg, bg, sp, gn, l, bsz, seq, tm):
    ns = seq // tm
    n = bsz * seq
    tok = lambda rows, width: pl.BlockSpec((rows, width), lambda b, s: (b * ns + s, 0))
    heads = pl.BlockSpec((1, B_HEADS, tm, 128), lambda b, s: (b, 0, s, 0))
    return pl.pallas_call(
        _in_proj_kernel,
        grid=(bsz, ns),
        in_specs=[tok(tm, D_MODEL), _layer_spec(wa, l), _layer_spec(wb, l),
                  _layer_spec(bf_row, l), _const_spec((1, 128)), _const_spec((B_HEADS, 128)), _const_spec(hsel.shape)]
                 + [_layer_spec(a, l) for a in (cw, cb, wg, bg, sp, gn)],
        out_specs=[tok(tm, D_A), heads, heads, tok(tm, D_B), tok(tm // S5_CHUNK, S5_CHUNK * D_C),
                   pl.BlockSpec((1, 1, HEAD_PAIRS, 8, 128), lambda b, s: (b, s, 0, 0, 0)),
                   pl.BlockSpec((1, 1, 8, 128), lambda b, s: (b, s, 0, 0))],
        out_shape=[jax.ShapeDtypeStruct((n, D_A), BF16),
                   jax.ShapeDtypeStruct((bsz, B_HEADS, seq, 128), BF16),
                   jax.ShapeDtypeStruct((bsz, B_HEADS, seq, 128), BF16),
                   jax.ShapeDtypeStruct((n, D_B), BF16),
                   jax.ShapeDtypeStruct((n // S5_CHUNK, S5_CHUNK * D_C), F32),
                   jax.ShapeDtypeStruct((bsz, ns, HEAD_PAIRS, 8, 128), F32),
                   jax.ShapeDtypeStruct((bsz, ns, 8, 128), F32)],
        scratch_shapes=[pltpu.VMEM((1, 128), F32), pltpu.VMEM((8, D_A), F32), pltpu.VMEM((1, D_A), F32),
                        pltpu.VMEM((tm, D_A), F32), pltpu.VMEM((2, tm, 128), F32), pltpu.VMEM((ns, 8, 128), F32)],
        compiler_params=_cparams("parallel", "arbitrary"),
        name="in_proj",
    )(x2d, wa, wb, bf_row, fmask, ones, hsel, cw, cb, wg, bg, sp, gn)


def _fox_kernel(skip_ref, qa_ref, ka_ref, v_ref, base_ref, o_ref, vaug_sc, acc_sc, m_sc, *, tq):
    seq = v_ref.shape[1]
    lane = lax.broadcasted_iota(jnp.int32, (ROWS, 128), 1)
    sub = lax.broadcasted_iota(jnp.int32, (ROWS, 128), 0)

    lane_v = lax.broadcasted_iota(jnp.int32, (tq, 128), 1)
    for c in range(seq // tq):
        vc = v_ref[0, c * tq:(c + 1) * tq, :].astype(F32)
        vaug_sc[0, c * tq:(c + 1) * tq, :] = jnp.concatenate(
            [jnp.where(lane_v < B_HEAD_DIM, vc, 0.0), jnp.where(lane_v == 0, 1.0, 0.0)], axis=1).astype(BF16)
        vaug_sc[1, c * tq:(c + 1) * tq, :] = jnp.concatenate(
            [jnp.where(lane_v >= B_HEAD_DIM, vc, 0.0), jnp.where(lane_v == 1, 1.0, 0.0)], axis=1).astype(BF16)

    def query_block(qi, carry):
        first = skip_ref[pl.program_id(0), pl.program_id(1), qi]
        past = qi - first
        q0 = pl.multiple_of(qi * tq, tq)
        acc_sc[...] = jnp.zeros_like(acc_sc)
        m_sc[...] = jnp.full_like(m_sc, NEG)
        base_q = base_ref[0, qi, 0]

        def tasks_of(k0, base_k, diag):
            deltas = [base_q[j:j + 1, :] - base_k[j:j + 1, :] for j in range(2)]
            return [(k0, deltas, diag, r) for r in range(tq // ROWS)]

        def qk(task):
            k0, _, diag, r = task
            n = (r + 1) * ROWS if diag else tq
            rows = pl.ds(pl.multiple_of(q0 + r * ROWS, ROWS), ROWS)
            return [lax.dot_general(qa_ref[0, j, rows, :], ka_ref[0, j, pl.ds(k0, n), :],
                                    (((1,), (1,)), ((), ())), preferred_element_type=F32) for j in range(2)]

        def finish(task, s_pair):
            k0, deltas, diag, r = task
            rows = slice(r * ROWS, (r + 1) * ROWS)
            n = (r + 1) * ROWS if diag else tq
            ps, alphas = [], []
            for j in range(2):
                s = s_pair[j]
                tiles = [s[:, c * 128:(c + 1) * 128] for c in range(n // 128)]
                if diag:
                    tiles[-1] = jnp.where(lane <= sub, tiles[-1], NEG)
                tmax = tiles[0]
                for t in tiles[1:]:
                    tmax = jnp.maximum(tmax, t)
                m_cur = jnp.max(tmax, axis=-1, keepdims=True) + deltas[j]
                m_prev = m_sc[j, rows, :]
                m_new = jnp.maximum(m_prev, m_cur)
                alphas.append(jnp.exp2(m_prev - m_new))
                m_sc[j, rows, :] = m_new
                mt = m_new - deltas[j]
                ps.append(jnp.concatenate([jnp.exp2(t - mt) for t in tiles], axis=1).astype(BF16))
            vk = jnp.concatenate([vaug_sc[0, pl.ds(k0, n), :], vaug_sc[1, pl.ds(k0, n), :]], axis=0)
            pv = jnp.dot(jnp.concatenate(ps, axis=1), vk, preferred_element_type=F32)
            a = jnp.concatenate([jnp.where(lane < B_HEAD_DIM, alphas[0], alphas[1]),
                                 jnp.where(lane == 0, alphas[0], alphas[1])], axis=1)
            acc_sc[rows, :] = a * acc_sc[rows, :] + pv

        def run(tasks):
            s_next = qk(tasks[0])
            for i, task in enumerate(tasks):
                s_cur = s_next
                if i + 1 < len(tasks):
                    s_next = qk(tasks[i + 1])
                finish(task, s_cur)

        def past_tasks(ki):
            return tasks_of(pl.multiple_of(ki * tq, tq), base_ref[0, ki, 0], False)

        def quad(i, c):
            run([t for u in range(4) for t in past_tasks(first + 4 * i + u)])
            return c

        lax.fori_loop(0, past // 4, quad, 0)
        rem0 = first + (past // 4) * 4

        for tail in range(4):
            @pl.when(past % 4 == tail)
            def _(tail=tail):
                run([t for u in range(tail) for t in past_tasks(rem0 + u)] + tasks_of(q0, base_q, True))
                acc = acc_sc[...]
                lane_o = lax.broadcasted_iota(jnp.int32, (tq, 128), 1)
                o_ref[0, pl.ds(q0, tq), :] = acc[:, :128] / jnp.where(lane_o < B_HEAD_DIM, acc[:, 128:129], acc[:, 129:130])

        return carry

    lax.fori_loop(0, seq // tq, query_block, 0)


def _fox(skip, qa, ka, v3, base, tq):
    bsz, seq, _ = v3.shape
    ns = seq // tq
    heads = pl.BlockSpec((1, 2, seq, 128), lambda b, h, sk: (b, h, 0, 0))
    return pl.pallas_call(
        functools.partial(_fox_kernel, tq=tq),
        grid_spec=pltpu.PrefetchScalarGridSpec(
            num_scalar_prefetch=1,
            grid=(bsz, HEAD_PAIRS),
            in_specs=[heads, heads,
                      pl.BlockSpec((1, seq, 128), lambda b, h, sk: (b, 0, h)),
                      pl.BlockSpec((1, ns, 1, 8, 128), lambda b, h, sk: (b, 0, h, 0, 0))],
            out_specs=pl.BlockSpec((1, seq, 128), lambda b, h, sk: (b, 0, h)),
            scratch_shapes=[pltpu.VMEM((2, seq, 256), BF16), pltpu.VMEM((tq, 256), F32),
                            pltpu.VMEM((2, tq, 128), F32)]),
        out_shape=jax.ShapeDtypeStruct((bsz, seq, D_B), F32),
        compiler_params=_cparams("parallel", "parallel"),
        name="fox_attention",
    )(skip, qa, ka, v3, base)


def _s5_kernel(u_ref, m_ref, bm_ref, cm_ref, lev_ref, pw_ref, d_ref, wglu_ref, gn_ref, o_ref, hs_sc):
    u = u_ref[...]
    rows = u.shape[0]
    groups = rows // 8
    ub = u.astype(BF16)
    sub8 = lax.broadcasted_iota(jnp.int32, (8, S5_HSTATE), 0)
    y_halves = []
    for h in range(2):
        re_cols = slice((2 * h) * S5_HSTATE, (2 * h + 1) * S5_HSTATE)
        im_cols = slice((2 * h + 1) * S5_HSTATE, (2 * h + 2) * S5_HSTATE)
        uh = jnp.concatenate([ub[:, t * D_C + h * 128: t * D_C + (h + 1) * 128] for t in range(S5_CHUNK)],
                             axis=1)
        y = jnp.dot(uh, m_ref[h], preferred_element_type=F32)
        st = jnp.dot(uh, bm_ref[h], preferred_element_type=F32)
        re3 = st[:, :S5_HSTATE].reshape(groups, 8, S5_HSTATE)
        im3 = st[:, S5_HSTATE:].reshape(groups, 8, S5_HSTATE)
        for lv, d in enumerate((1, 2, 4)):
            ar = jnp.where(sub8 >= d, lev_ref[lv:lv + 1, re_cols], 0.0)
            ai = jnp.where(sub8 >= d, lev_ref[lv:lv + 1, im_cols], 0.0)
            rr = pltpu.roll(re3, d, axis=1)
            ri = pltpu.roll(im3, d, axis=1)
            re3, im3 = re3 + (ar * rr - ai * ri), im3 + (ar * ri + ai * rr)
        pr, pi = pw_ref[:, re_cols], pw_ref[:, im_cols]
        cre = jnp.zeros((1, S5_HSTATE), F32)
        cim = jnp.zeros((1, S5_HSTATE), F32)
        for g in range(groups):
            hr = re3[g] + (pr * cre - pi * cim)
            hi = im3[g] + (pr * cim + pi * cre)
            hs_sc[0, 8 * g:8 * g + 8, :] = hr
            hs_sc[1, 8 * g:8 * g + 8, :] = hi
            cre, cim = hr[7:8, :], hi[7:8, :]
        hprev = jnp.concatenate([_shift_rows(hs_sc[0], 1), _shift_rows(hs_sc[1], 1)], axis=1)
        y = y + jnp.dot(hprev.astype(BF16), cm_ref[h], preferred_element_type=F32)
        y_halves.append(y)
    outs = []
    for t in range(S5_CHUNK):
        yt = jnp.concatenate([y_halves[0][:, t * 128:(t + 1) * 128], y_halves[1][:, t * 128:(t + 1) * 128]], axis=1)
        yt = _gelu(yt + d_ref[...] * u[:, t * D_C:(t + 1) * D_C])
        gl = jnp.dot(yt.astype(BF16), wglu_ref[...], preferred_element_type=F32)
        outs.append(_rms_norm(yt * jax.nn.sigmoid(gl), gn_ref[...]))
    o_ref[...] = jnp.concatenate(outs, axis=1)


def _s5(u2, m, bm, cm, lev, pw, d, wglu, gn, l, rows_per_seq):
    n, width = u2.shape
    blk = pl.BlockSpec((rows_per_seq, width), lambda b: (b, 0))
    return pl.pallas_call(
        _s5_kernel,
        grid=(n // rows_per_seq,),
        in_specs=[blk] + [_layer_spec(a, l) for a in (m, bm, cm, lev, pw, d, wglu, gn)],
        out_specs=blk,
        out_shape=jax.ShapeDtypeStruct(u2.shape, F32),
        scratch_shapes=[pltpu.VMEM((2, rows_per_seq, S5_HSTATE), F32)],
        compiler_params=_cparams("parallel"),
        name="s5",
    )(u2, m, bm, cm, lev, pw, d, wglu, gn)


def _s5_prepare(a_re, a_im, log_dt, b_re, b_im, c_re, c_im):
    hi = lax.Precision.HIGHEST
    T = S5_CHUNK
    nl = a_re.shape[0]
    dt = jnp.exp(log_dt)[:, :, None]
    lr, li = a_re * dt, a_im * dt

    def cpow(tau):
        mag = jnp.exp(lr[:, None] * tau[None, :, None, None])
        ang = li[:, None] * tau[None, :, None, None]
        return mag * jnp.cos(ang), mag * jnp.sin(ang)

    def halves(z):
        return jnp.moveaxis(z.reshape(z.shape[:2] + (2, S5_HALF) + z.shape[3:]), 2, 1)

    ar1, ai1 = cpow(jnp.ones((1,), F32))
    nr, ni = ar1[:, 0] - 1.0, ai1[:, 0]
    den = a_re * a_re + a_im * a_im
    fr, fi = (nr * a_re + ni * a_im) / den, (ni * a_re - nr * a_im) / den
    bbr = fr[..., None] * b_re - fi[..., None] * b_im
    bbi = fr[..., None] * b_im + fi[..., None] * b_re

    dr, di = cpow((T - 1) - jnp.arange(T, dtype=F32))
    abr = dr[..., None] * bbr[:, None] - di[..., None] * bbi[:, None]
    abi = dr[..., None] * bbi[:, None] + di[..., None] * bbr[:, None]

    def same_group(rows, rdiv, cols, cdiv, cmod=S5_HALF):
        r = (jnp.arange(rows) // rdiv)[:, None]
        c = ((jnp.arange(cols) // cdiv) % cmod)[None, :]
        return (r == c).astype(F32)

    def tile_eye(n, reps):
        return (jnp.arange(n)[:, None] == (jnp.arange(n * reps) % n)[None, :]).astype(F32)

    kern = (jnp.einsum('lgop,ljgpi->ljgio', c_re, abr, precision=hi)
            - jnp.einsum('lgop,ljgpi->ljgio', c_im, abi, precision=hi))
    kd = jnp.einsum('lhjrc,cn->lhjrn', halves(kern).reshape(nl, 2, T, 128, C_GROUP), tile_eye(C_GROUP, S5_HALF))
    kd = kd * same_group(128, C_GROUP, 128, C_GROUP)
    ext = jnp.concatenate([kd, jnp.zeros((nl, 2, T - 1, 128, 128), F32)], axis=2)
    m = jnp.concatenate([ext[:, :, T - 1 - b:2 * T - 1 - b] for b in range(T)], axis=-1)
    m = m.reshape(nl, 2, S5_HCOLS, S5_HCOLS)

    def state_in(ab):
        y = jnp.swapaxes(halves(ab), -1, -2).reshape(nl, 2, T, 128, C_STATE)
        return jnp.einsum('lhjrp,pn->lhjrn', y, tile_eye(C_STATE, S5_HALF)) * same_group(128, C_GROUP, S5_HSTATE, C_STATE)

    bm = jnp.concatenate([state_in(abr), state_in(abi)], axis=-1).reshape(nl, 2, S5_HCOLS, 2 * S5_HSTATE)

    ur, ui = cpow(jnp.arange(1, T + 1, dtype=F32))
    so_r = c_re[:, None] * ur[:, :, :, None, :] - c_im[:, None] * ui[:, :, :, None, :]
    so_i = c_re[:, None] * ui[:, :, :, None, :] + c_im[:, None] * ur[:, :, :, None, :]
    expand = (jnp.arange(T * C_GROUP)[:, None] // C_GROUP == jnp.arange(S5_HCOLS)[None, :] // 128) & (
        jnp.arange(T * C_GROUP)[:, None] % C_GROUP == jnp.arange(S5_HCOLS)[None, :] % C_GROUP)

    def state_out(so):
        y = jnp.transpose(halves(so), (0, 1, 3, 5, 2, 4)).reshape(nl, 2, S5_HSTATE, T * C_GROUP)
        return jnp.einsum('lhrc,cn->lhrn', y, expand.astype(F32)) * same_group(S5_HSTATE, C_STATE, S5_HCOLS, C_GROUP)

    cm = jnp.concatenate([state_out(so_r), state_out(-so_i)], axis=2)

    def state_layout(zr, zi):
        n = zr.shape[1]
        parts = []
        for h in range(2):
            gs = slice(h * S5_HALF, (h + 1) * S5_HALF)
            parts += [zr[:, :, gs].reshape(nl, n, S5_HSTATE), zi[:, :, gs].reshape(nl, n, S5_HSTATE)]
        return jnp.concatenate(parts, axis=-1)

    lev = state_layout(*cpow(jnp.asarray([T, 2 * T, 4 * T], F32)))
    pw = state_layout(*cpow(T * jnp.arange(1, 9, dtype=F32)))
    return m.astype(BF16), bm.astype(BF16), cm.astype(BF16), lev, pw


def _mix_ffn_kernel(oa_ref, ob_ref, oc_ref, x_ref, gb_ref, wo_ref, g2_ref, b2_ref, wg_ref, wu_ref, wd_ref,
                    g3_ref, b3_ref, o_ref, oc_sc):
    tm = x_ref.shape[0]
    for t in range(S5_CHUNK):
        for j in range(2):
            oc_sc[j, pl.ds(t, tm // S5_CHUNK, stride=S5_CHUNK), :] = oc_ref[:, t * D_C + j * 128:t * D_C + (j + 1) * 128]
    chains = _chain_rows(tm)
    mixes = []
    for rows in chains:
        nb = _rms_norm(ob_ref[rows, :], gb_ref[...]).astype(BF16)
        o = jnp.concatenate([oa_ref[rows, :], nb, oc_sc[0, rows, :].astype(BF16), oc_sc[1, rows, :].astype(BF16)], axis=1)
        mixes.append(jnp.dot(o, wo_ref[...], preferred_element_type=F32))
    xs = [_layer_norm(ALPHA * x_ref[rows, :] + mix, g2_ref[...], b2_ref[...]) for rows, mix in zip(chains, mixes)]
    for rows, out in zip(chains, _ffn_ln_chains(xs, wg_ref, wu_ref, wd_ref, g3_ref, b3_ref)):
        o_ref[rows, :] = out


def _mix_ffn(oa2, ob2, oc2, x2d, gb, wo, g2, b2, wg, wu, wd, g3, b3, l, tm):
    tok = lambda rows, width: pl.BlockSpec((rows, width), lambda i: (i, 0))
    return pl.pallas_call(
        _mix_ffn_kernel,
        grid=(x2d.shape[0] // tm,),
        in_specs=[tok(tm, D_A), tok(tm, D_B), tok(tm // S5_CHUNK, S5_CHUNK * D_C), tok(tm, D_MODEL)]
                 + [_layer_spec(a, l) for a in (gb, wo, g2, b2, wg, wu, wd, g3, b3)],
        out_specs=tok(tm, D_MODEL),
        out_shape=jax.ShapeDtypeStruct(x2d.shape, F32),
        scratch_shapes=[pltpu.VMEM((2, tm, 128), F32)],
        compiler_params=_cparams("parallel"),
        name="mix_ffn_ln",
    )(oa2, ob2, oc2, x2d, gb, wo, g2, b2, wg, wu, wd, g3, b3)


def _block_diag(w):
    eye = jnp.eye(A_BLOCKS, dtype=w.dtype)
    return jnp.einsum('lhij,hk->lhikj', w, eye).reshape(w.shape[0], D_A, D_A)


def _pick(n, pref):
    t = min(n, pref)
    while n % t:
        t //= 2
    return t


def kernel(x, ffn1_w_gate, ffn1_w_up, ffn1_w_down, ln1_g, ln1_b, w_in, conv_w, conv_b, rg_w_a, rg_b_a, rg_w_x, rg_b_x, rg_lambda, fox_b_f, s5_a_re, s5_a_im, s5_log_dt, s5_b_re, s5_b_im, s5_c_re, s5_c_im, s5_d, s5_w_glu, mix_norm_g, w_out, ln2_g, ln2_b, ffn2_w_gate, ffn2_w_up, ffn2_w_down, ln3_g, ln3_b):
    bsz, seq, _ = x.shape
    n = bsz * seq
    tm = _pick(seq, 512)
    vec = lambda v: v[:, None, :]
    bf = lambda w: w.astype(BF16)

    f1, f2 = [bf(w) for w in (ffn1_w_gate, ffn1_w_up, ffn1_w_down)], [bf(w) for w in (ffn2_w_gate, ffn2_w_up, ffn2_w_down)]
    cuts = np.cumsum([D_A, D_A, D_B, D_B, D_B, B_HEADS])
    wa = bf(w_in[:, :, :W_SPLIT])
    wt = bf(w_in[:, :, W_SPLIT:])
    place = np.zeros((B_HEADS, 128), np.float32)
    for piece in range(3):
        place[np.arange(B_HEADS), F_LANE0 + 8 * piece + np.arange(B_HEADS)] = 1.0
    place = jnp.asarray(place)
    hsel = np.zeros((2 * HEAD_PAIRS * 128, 128), np.float32)
    for p in range(HEAD_PAIRS):
        for is_k in range(2):
            for e in range(2):
                r0 = (2 * p + is_k) * 128 + e * B_HEAD_DIM
                hsel[r0:r0 + B_HEAD_DIM, F_LANE0 + 8 * is_k + 2 * p + e] = 1.0
    hsel = jnp.asarray(hsel, BF16)
    fcols = jnp.einsum('ldh,hn->ldn', wt[:, :, cuts[4] - W_SPLIT:cuts[5] - W_SPLIT], place.astype(BF16))
    wb = jnp.concatenate([wt[:, :, :cuts[4] - W_SPLIT], wt[:, :, cuts[5] - W_SPLIT:], fcols], axis=-1)
    bf_row = jnp.einsum('lh,hn->ln', fox_b_f, place, precision=lax.Precision.HIGHEST)[:, None, :]
    fmask = jnp.max(place, axis=0, keepdims=True)
    wgates = bf(jnp.concatenate([_block_diag(rg_w_a), _block_diag(rg_w_x)], axis=-1))
    bgates = jnp.concatenate([rg_b_a, rg_b_x], axis=-1)[:, None, :]
    g_a, g_b, g_c = (vec(g) for g in jnp.split(mix_norm_g, [D_A, D_A + D_B], axis=-1))
    sp = vec(jax.nn.softplus(-rg_lambda))
    m, bm, cm, lev, pw = _s5_prepare(s5_a_re, s5_a_im, s5_log_dt, s5_b_re, s5_b_im, s5_c_re, s5_c_im)
    wglu, wo = bf(s5_w_glu), bf(w_out)

    h = x.reshape(n, D_MODEL)
    for l in range(DEPTH):
        h = _ffn_ln(h, f1[0], f1[1], f1[2], vec(ln1_g), vec(ln1_b), l, tm)
        oa, qa, ka, v, cu, base, skip = _in_proj(h, wa, wb, bf_row, fmask, place, hsel, conv_w, vec(conv_b), wgates,
                                                 bgates, sp, g_a, l, bsz, seq, tm)
        skip = jnp.transpose(skip[:, :, 0, F_LANE0:F_LANE0 + B_HEADS:2], (0, 2, 1)).astype(jnp.int32)
        ob = _fox(skip, qa, ka, v.reshape(bsz, seq, D_B), base, tm)
        oc = _s5(cu, m, bm, cm, lev, pw, vec(s5_d), wglu, g_c, l, seq // S5_CHUNK)
        h = _mix_ffn(oa, ob.reshape(n, D_B), oc, h, g_b, wo, vec(ln2_g), vec(ln2_b), f2[0], f2[1], f2[2],
                     vec(ln3_g), vec(ln3_b), l, tm)
    return h.reshape(bsz, seq, D_MODEL)
```

```python
import functools
import math

import numpy as np
import jax
import jax.numpy as jnp
from jax import lax
from jax.experimental import pallas as pl
from jax.experimental.pallas import tpu as pltpu

F32 = jnp.float32
BF16 = jnp.bfloat16

D_MODEL = 1024
D_FF = 2816
D_A = 384
A_BLOCKS = 6
A_BLOCK_DIM = 64
CONV_WIDTH = 4
RG_C = 8.0
D_B = 384
B_HEADS = 6
B_HEAD_DIM = 64
HEAD_PAIRS = B_HEADS // 2
D_C = 256
C_GROUP = 16
C_GROUPS = 16
C_STATE = 64
W_SPLIT = 1792
W_TAIL = 2 * D_A + 3 * D_B - W_SPLIT + D_C + 128
F_LANE0 = 64
LOG2E = math.log2(math.e)
QSCALE = LOG2E * B_HEAD_DIM ** -0.5
NEG = -1e30
SKIP_BELOW = -160.0
NORM_GUARD = 1.0 + 2.0 ** -7
ROWS = 128
FFN_ROWS = 256
DEPTH = 2
ALPHA = (2 * DEPTH) ** 0.25
LN_EPS = 1e-5
RMS_EPS = 1e-6

S5_CHUNK = 8
S5_HALF = C_GROUPS // 2
S5_HCOLS = S5_CHUNK * S5_HALF * C_GROUP
S5_HSTATE = S5_HALF * C_STATE

VMEM_LIMIT = 56 * 1024 * 1024


def _cparams(*sem):
    return pltpu.CompilerParams(dimension_semantics=sem, vmem_limit_bytes=VMEM_LIMIT)


def _const_spec(shape):
    nd = len(shape)
    return pl.BlockSpec(shape, lambda *_: (0,) * nd, pipeline_mode=pl.Buffered(1))


def _layer_spec(arr, l, block=None):
    block = tuple(arr.shape[1:]) if block is None else block
    nd = len(block)
    return pl.BlockSpec((None,) + block, lambda *_: (l,) + (0,) * nd, pipeline_mode=pl.Buffered(1))


def _layer_norm(r, g, b):
    mu = jnp.mean(r, axis=-1, keepdims=True)
    d = r - mu
    var = jnp.mean(d * d, axis=-1, keepdims=True)
    return d * lax.rsqrt(var + LN_EPS) * g + b


def _rms_norm(x, g):
    return x * lax.rsqrt(jnp.mean(x * x, axis=-1, keepdims=True) + RMS_EPS) * g


def _gelu(x):
    return 0.5 * x * (1.0 + jnp.tanh(math.sqrt(2.0 / math.pi) * (x + 0.044715 * (x * x * x))))


def _log_sigmoid(x):
    return -(jnp.maximum(-x, 0.0) + jnp.log1p(jnp.exp(-jnp.abs(x))))


def _shift_rows(x, d):
    rows, cols = x.shape
    groups = rows // 8
    x3 = x.reshape(groups, 8, cols)
    if d % 8 == 0:
        return jnp.concatenate([jnp.zeros((d // 8, 8, cols), x.dtype), x3[:groups - d // 8]], axis=0).reshape(rows, cols)
    sub = lax.broadcasted_iota(jnp.int32, (groups, 8, cols), 1)
    rolled = pltpu.roll(x3, d, axis=1)
    before = jnp.concatenate([jnp.zeros((1, 8, cols), x.dtype), rolled[:groups - 1]], axis=0)
    return jnp.where(sub >= d, rolled, before).reshape(rows, cols)


def _ffn_ln_chains(xs, wg_ref, wu_ref, wd_ref, g_ref, b_ref):
    xbs = [x.astype(BF16) for x in xs]
    gus = [(jnp.dot(xb, wg_ref[...], preferred_element_type=F32), jnp.dot(xb, wu_ref[...], preferred_element_type=F32))
           for xb in xbs]
    ys = [jnp.dot((gate * jax.nn.sigmoid(gate) * up).astype(BF16), wd_ref[...], preferred_element_type=F32)
          for gate, up in gus]
    return [_layer_norm(ALPHA * x + 0.5 * y, g_ref[...], b_ref[...]) for x, y in zip(xs, ys)]


def _chain_rows(n):
    return [slice(c * FFN_ROWS, (c + 1) * FFN_ROWS) for c in range(n // FFN_ROWS)]


def _ffn_ln_kernel(x_ref, wg_ref, wu_ref, wd_ref, g_ref, b_ref, o_ref):
    chains = _chain_rows(x_ref.shape[0])
    for pair in range(0, len(chains), 2):
        group = chains[pair:pair + 2]
        outs = _ffn_ln_chains([x_ref[rows, :] for rows in group], wg_ref, wu_ref, wd_ref, g_ref, b_ref)
        for rows, out in zip(group, outs):
            o_ref[rows, :] = out


def _ffn_ln(x2d, wg, wu, wd, g, b, l, tm):
    n = x2d.shape[0]
    row = pl.BlockSpec((tm, D_MODEL), lambda i: (i, 0))
    return pl.pallas_call(
        _ffn_ln_kernel,
        grid=(n // tm,),
        in_specs=[row] + [_layer_spec(a, l) for a in (wg, wu, wd, g, b)],
        out_specs=row,
        out_shape=jax.ShapeDtypeStruct((n, D_MODEL), F32),
        compiler_params=_cparams("parallel"),
        name="ffn_ln",
    )(x2d, wg, wu, wd, g, b)


def _rglru_gates(ax, cw_ref, cb_ref, wg_ref, bg_ref, prev8_sc):
    tm = ax.shape[0]
    groups = tm // 8
    sub = lax.broadcasted_iota(jnp.int32, (groups, 8, D_A), 1)
    ax3 = ax.reshape(groups, 8, D_A)
    prev8 = prev8_sc[...]
    xa = cb_ref[...] + cw_ref[CONV_WIDTH - 1:CONV_WIDTH, :] * ax3
    for k in range(1, CONV_WIDTH):
        rolled = pltpu.roll(ax3, k, axis=1)
        before = jnp.concatenate([pltpu.roll(prev8, k, axis=0)[None], rolled[:groups - 1]], axis=0)
        xa = xa + cw_ref[CONV_WIDTH - 1 - k:CONV_WIDTH - k, :] * jnp.where(sub >= k, rolled, before)
    prev8_sc[...] = ax3[groups - 1]
    xa = xa.reshape(tm, D_A)
    return xa, jnp.dot(xa.astype(BF16), wg_ref[...], preferred_element_type=F32) + bg_ref[...]


def _rglru_scan(xa, gates, ag, sp_ref, gn_ref, hcarry_sc, hs_sc):
    tm = xa.shape[0]
    groups = tm // 8
    grouped = lambda v: v.reshape(groups, 8, D_A)
    sub = lax.broadcasted_iota(jnp.int32, (groups, 8, D_A), 1)
    r = jax.nn.sigmoid(gates[:, :D_A])
    i = jax.nn.sigmoid(gates[:, D_A:])
    log_a = (-RG_C) * r * sp_ref[...]
    a = jnp.exp(log_a)
    y = -jnp.tanh(log_a) * (1.0 + a * a)
    b = jnp.where(y > 0.0, y * lax.rsqrt(y), 0.0) * (i * xa)

    a3, b3 = grouped(a), grouped(b)
    for d in (1, 2, 4):
        valid = sub >= d
        b3 = jnp.where(valid, a3 * pltpu.roll(b3, d, axis=1) + b3, b3)
        a3 = jnp.where(valid, a3 * pltpu.roll(a3, d, axis=1), a3)
    last = hcarry_sc[...]
    for g in range(groups):
        hg = b3[g] + a3[g] * last
        hs_sc[8 * g:8 * g + 8, :] = hg
        last = hg[7:8, :]
    hcarry_sc[...] = last
    return _rms_norm(_gelu(ag) * hs_sc[...], gn_ref[...])


def _in_proj_kernel(x_ref, wa_ref, wb_ref, bf_ref, fmask_ref, ones_ref, hsel_ref, cw_ref, cb_ref, wg_ref, bg_ref, sp_ref,
                    gn_ref, oa_ref, qa_ref, ka_ref, v_ref, cu_ref, base_ref, skip_ref,
                    carry_sc, prev8_sc, hcarry_sc, hs_sc, cu_sc, hist_sc):
    tm = x_ref.shape[0]
    step = pl.program_id(1)

    @pl.when(step == 0)
    def _():
        carry_sc[...] = jnp.zeros_like(carry_sc)
        prev8_sc[...] = jnp.zeros_like(prev8_sc)
        hcarry_sc[...] = jnp.zeros_like(hcarry_sc)
        hist_sc[...] = jnp.zeros_like(hist_sc)

    xb = x_ref[...].astype(BF16)
    o = 2 * D_A
    v_rest = W_SPLIT - (o + 2 * D_B)
    za = jnp.dot(xb, wa_ref[:, 0:o], preferred_element_type=F32)
    zb = jnp.dot(xb, wb_ref[...], preferred_element_type=F32)
    xa, gates = _rglru_gates(za[:, 0:D_A], cw_ref, cb_ref, wg_ref, bg_ref, prev8_sc)

    cu = zb[:, D_B - v_rest:D_B - v_rest + D_C]
    cu_sc[0] = cu[:, :128]
    cu_sc[1] = cu[:, 128:]
    cu_ref[...] = jnp.concatenate([cu_sc[j, pl.ds(t, tm // S5_CHUNK, stride=S5_CHUNK), :]
                                   for t in range(S5_CHUNK) for j in range(2)], axis=1)

    lf = _log_sigmoid(zb[:, W_TAIL - 128:] + bf_ref[...]) * fmask_ref[...] * LOG2E
    lane = lax.broadcasted_iota(jnp.int32, (tm, 128), 1)
    c = lf
    d = 1
    while d < tm:
        c = c + _shift_rows(c, d)
        d *= 2
    carry = carry_sc[...]
    rows = [jnp.broadcast_to(carry[:, F_LANE0 + h:F_LANE0 + h + 1], (1, 128)) for h in range(B_HEADS)]
    pad = jnp.zeros((6, 128), F32)
    base_ref[0, 0] = jnp.stack([jnp.concatenate([rows[2 * p], rows[2 * p + 1], pad], axis=0)
                                for p in range(HEAD_PAIRS)])
    carry_sc[...] = carry + c[tm - 1:tm, :]
    hi = c.astype(BF16).astype(F32)
    r1 = c - hi
    mid = r1.astype(BF16).astype(F32)
    lo = (r1 - mid).astype(BF16).astype(F32)
    pieces = jnp.where(lane < F_LANE0 + 8, hi, jnp.where(lane < F_LANE0 + 16, mid, lo))

    zq = jnp.dot(xb, wa_ref[:, o:W_SPLIT], preferred_element_type=F32)
    oa_ref[...] = _rglru_scan(xa, gates, za[:, D_A:], sp_ref, gn_ref, hcarry_sc, hs_sc).astype(BF16)

    v_ref[...] = jnp.concatenate([zq[:, 2 * D_B:], zb[:, :D_B - v_rest]], axis=1).astype(BF16)
    sqs = []
    for p in range(HEAD_PAIRS):
        qt = zq[:, 128 * p:128 * (p + 1)]
        kt = zq[:, D_B + 128 * p:D_B + 128 * (p + 1)]
        sqs += [qt * qt, kt * kt]
        for e in range(2):
            h = 2 * p + e
            qh = pltpu.roll(qt, B_HEAD_DIM, axis=1) if e else qt
            kh = pltpu.roll(kt, B_HEAD_DIM, axis=1) if e else kt
            qa_ref[0, h] = jnp.where(lane < B_HEAD_DIM, qh * QSCALE, ones_ref[h:h + 1, :]).astype(BF16)
            ka_ref[0, h] = jnp.where(lane < B_HEAD_DIM, kh, -pieces).astype(BF16)
    norms = jnp.dot(jnp.concatenate(sqs, axis=1).astype(BF16), hsel_ref[...], preferred_element_type=F32)
    top = jnp.max(norms, axis=0, keepdims=True)
    qn2 = top
    kn2 = pltpu.roll(top, 120, axis=1)

    qn = jnp.sqrt(qn2) * (QSCALE * NORM_GUARD)
    kn_here = jnp.sqrt(kn2)
    alive = jnp.ones((1, 128), F32)
    count = jnp.zeros((1, 128), F32)
    for kb in range(hist_sc.shape[0]):
        bound = qn * (jnp.sqrt(hist_sc[kb, 1:2, :]) + kn_here) + (carry - hist_sc[kb, 0:1, :])
        ok = jnp.where(bound <= SKIP_BELOW, 1.0, 0.0)
        ok = ok * pltpu.roll(ok, 127, axis=1)
        alive = alive * jnp.where(kb < step, ok, 0.0)
        count = count + alive
    skip_ref[0, 0] = jnp.broadcast_to(count, (8, 128))
    hist_sc[step, 0:1, :] = carry + c[tm - 1:tm, :]
    hist_sc[step, 1:2, :] = kn2


def _in_proj(x2d, wa, wb, bf_row, fmask, ones, hsel, cw, cb, wg, bg, sp, gn, l, bsz, seq, tm):
    ns = seq // tm
    n = bsz * seq
    tok = lambda rows, width: pl.BlockSpec((rows, width), lambda b, s: (b * ns + s, 0))
    heads = pl.BlockSpec((1, B_HEADS, tm, 128), lambda b, s: (b, 0, s, 0))
    return pl.pallas_call(
        _in_proj_kernel,
        grid=(bsz, ns),
        in_specs=[tok(tm, D_MODEL), _layer_spec(wa, l), _layer_spec(wb, l),
                  _layer_spec(bf_row, l), _const_spec((1, 128)), _const_spec((B_HEADS, 128)), _const_spec(hsel.shape)]
                 + [_layer_spec(a, l) for a in (cw, cb, wg, bg, sp, gn)],
        out_specs=[tok(tm, D_A), heads, heads, tok(tm, D_B), tok(tm // S5_CHUNK, S5_CHUNK * D_C),
                   pl.BlockSpec((1, 1, HEAD_PAIRS, 8, 128), lambda b, s: (b, s, 0, 0, 0)),
                   pl.BlockSpec((1, 1, 8, 128), lambda b, s: (b, s, 0, 0))],
        out_shape=[jax.ShapeDtypeStruct((n, D_A), BF16),
                   jax.ShapeDtypeStruct((bsz, B_HEADS, seq, 128), BF16),
                   jax.ShapeDtypeStruct((bsz, B_HEADS, seq, 128), BF16),
                   jax.ShapeDtypeStruct((n, D_B), BF16),
                   jax.ShapeDtypeStruct((n // S5_CHUNK, S5_CHUNK * D_C), F32),
                   jax.ShapeDtypeStruct((bsz, ns, HEAD_PAIRS, 8, 128), F32),
                   jax.ShapeDtypeStruct((bsz, ns, 8, 128), F32)],
        scratch_shapes=[pltpu.VMEM((1, 128), F32), pltpu.VMEM((8, D_A), F32), pltpu.VMEM((1, D_A), F32),
                        pltpu.VMEM((tm, D_A), F32), pltpu.VMEM((2, tm, 128), F32), pltpu.VMEM((ns, 8, 128), F32)],
        compiler_params=_cparams("parallel", "arbitrary"),
        name="in_proj",
    )(x2d, wa, wb, bf_row, fmask, ones, hsel, cw, cb, wg, bg, sp, gn)


def _fox_kernel(skip_ref, qa_ref, ka_ref, v_ref, base_ref, o_ref, vaug_sc, acc_sc, m_sc, *, tq):
    seq = v_ref.shape[1]
    lane = lax.broadcasted_iota(jnp.int32, (ROWS, 128), 1)
    sub = lax.broadcasted_iota(jnp.int32, (ROWS, 128), 0)

    lane_v = lax.broadcasted_iota(jnp.int32, (tq, 128), 1)
    for c in range(seq // tq):
        vc = v_ref[0, c * tq:(c + 1) * tq, :].astype(F32)
        vaug_sc[0, c * tq:(c + 1) * tq, :] = jnp.concatenate(
            [jnp.where(lane_v < B_HEAD_DIM, vc, 0.0), jnp.where(lane_v == 0, 1.0, 0.0)], axis=1).astype(BF16)
        vaug_sc[1, c * tq:(c + 1) * tq, :] = jnp.concatenate(
            [jnp.where(lane_v >= B_HEAD_DIM, vc, 0.0), jnp.where(lane_v == 1, 1.0, 0.0)], axis=1).astype(BF16)

    def query_block(qi, carry):
        first = skip_ref[pl.program_id(0), pl.program_id(1), qi]
        past = qi - first
        q0 = pl.multiple_of(qi * tq, tq)
        acc_sc[...] = jnp.zeros_like(acc_sc)
        m_sc[...] = jnp.full_like(m_sc, NEG)
        base_q = base_ref[0, qi, 0]

        def tasks_of(k0, base_k, diag):
            deltas = [base_q[j:j + 1, :] - base_k[j:j + 1, :] for j in range(2)]
            return [(k0, deltas, diag, r) for r in range(tq // ROWS)]

        def qk(task):
            k0, _, diag, r = task
            n = (r + 1) * ROWS if diag else tq
            rows = pl.ds(pl.multiple_of(q0 + r * ROWS, ROWS), ROWS)
            return [lax.dot_general(qa_ref[0, j, rows, :], ka_ref[0, j, pl.ds(k0, n), :],
                                    (((1,), (1,)), ((), ())), preferred_element_type=F32) for j in range(2)]

        def finish(task, s_pair):
            k0, deltas, diag, r = task
            rows = slice(r * ROWS, (r + 1) * ROWS)
            n = (r + 1) * ROWS if diag else tq
            ps, alphas = [], []
            for j in range(2):
                s = s_pair[j]
                tiles = [s[:, c * 128:(c + 1) * 128] for c in range(n // 128)]
                if diag:
                    tiles[-1] = jnp.where(lane <= sub, tiles[-1], NEG)
                tmax = tiles[0]
                for t in tiles[1:]:
                    tmax = jnp.maximum(tmax, t)
                m_cur = jnp.max(tmax, axis=-1, keepdims=True) + deltas[j]
                m_prev = m_sc[j, rows, :]
                m_new = jnp.maximum(m_prev, m_cur)
                alphas.append(jnp.exp2(m_prev - m_new))
                m_sc[j, rows, :] = m_new
                mt = m_new - deltas[j]
                ps.append(jnp.concatenate([jnp.exp2(t - mt) for t in tiles], axis=1).astype(BF16))
            vk = jnp.concatenate([vaug_sc[0, pl.ds(k0, n), :], vaug_sc[1, pl.ds(k0, n), :]], axis=0)
            pv = jnp.dot(jnp.concatenate(ps, axis=1), vk, preferred_element_type=F32)
            a = jnp.concatenate([jnp.where(lane < B_HEAD_DIM, alphas[0], alphas[1]),
                                 jnp.where(lane == 0, alphas[0], alphas[1])], axis=1)
            acc_sc[rows, :] = a * acc_sc[rows, :] + pv

        def run(tasks):
            s_next = qk(tasks[0])
            for i, task in enumerate(tasks):
                s_cur = s_next
                if i + 1 < len(tasks):
                    s_next = qk(tasks[i + 1])
                finish(task, s_cur)

        def past_tasks(ki):
            return tasks_of(pl.multiple_of(ki * tq, tq), base_ref[0, ki, 0], False)

        def quad(i, c):
            run([t for u in range(4) for t in past_tasks(first + 4 * i + u)])
            return c

        lax.fori_loop(0, past // 4, quad, 0)
        rem0 = first + (past // 4) * 4

        for tail in range(4):
            @pl.when(past % 4 == tail)
            def _(tail=tail):
                run([t for u in range(tail) for t in past_tasks(rem0 + u)] + tasks_of(q0, base_q, True))
                acc = acc_sc[...]
                lane_o = lax.broadcasted_iota(jnp.int32, (tq, 128), 1)
                o_ref[0, pl.ds(q0, tq), :] = acc[:, :128] / jnp.where(lane_o < B_HEAD_DIM, acc[:, 128:129], acc[:, 129:130])

        return carry

    lax.fori_loop(0, seq // tq, query_block, 0)


def _fox(skip, qa, ka, v3, base, tq):
    bsz, seq, _ = v3.shape
    ns = seq // tq
    heads = pl.BlockSpec((1, 2, seq, 128), lambda b, h, sk: (b, h, 0, 0))
    return pl.pallas_call(
        functools.partial(_fox_kernel, tq=tq),
        grid_spec=pltpu.PrefetchScalarGridSpec(
            num_scalar_prefetch=1,
            grid=(bsz, HEAD_PAIRS),
            in_specs=[heads, heads,
                      pl.BlockSpec((1, seq, 128), lambda b, h, sk: (b, 0, h)),
                      pl.BlockSpec((1, ns, 1, 8, 128), lambda b, h, sk: (b, 0, h, 0, 0))],
            out_specs=pl.BlockSpec((1, seq, 128), lambda b, h, sk: (b, 0, h)),
            scratch_shapes=[pltpu.VMEM((2, seq, 256), BF16), pltpu.VMEM((tq, 256), F32),
                            pltpu.VMEM((2, tq, 128), F32)]),
        out_shape=jax.ShapeDtypeStruct((bsz, seq, D_B), F32),
        compiler_params=_cparams("parallel", "parallel"),
        name="fox_attention",
    )(skip, qa, ka, v3, base)


def _s5_kernel(u_ref, m_ref, bm_ref, cm_ref, lev_ref, pw_ref, d_ref, wglu_ref, gn_ref, o_ref, hs_sc):
    u = u_ref[...]
    rows = u.shape[0]
    groups = rows // 8
    ub = u.astype(BF16)
    sub8 = lax.broadcasted_iota(jnp.int32, (8, S5_HSTATE), 0)
    y_halves = []
    for h in range(2):
        re_cols = slice((2 * h) * S5_HSTATE, (2 * h + 1) * S5_HSTATE)
        im_cols = slice((2 * h + 1) * S5_HSTATE, (2 * h + 2) * S5_HSTATE)
        uh = jnp.concatenate([ub[:, t * D_C + h * 128: t * D_C + (h + 1) * 128] for t in range(S5_CHUNK)],
                             axis=1)
        y = jnp.dot(uh, m_ref[h], preferred_element_type=F32)
        st = jnp.dot(uh, bm_ref[h], preferred_element_type=F32)
        re3 = st[:, :S5_HSTATE].reshape(groups, 8, S5_HSTATE)
        im3 = st[:, S5_HSTATE:].reshape(groups, 8, S5_HSTATE)
        for lv, d in enumerate((1, 2, 4)):
            ar = jnp.where(sub8 >= d, lev_ref[lv:lv + 1, re_cols], 0.0)
            ai = jnp.where(sub8 >= d, lev_ref[lv:lv + 1, im_cols], 0.0)
            rr = pltpu.roll(re3, d, axis=1)
            ri = pltpu.roll(im3, d, axis=1)
            re3, im3 = re3 + (ar * rr - ai * ri), im3 + (ar * ri + ai * rr)
        pr, pi = pw_ref[:, re_cols], pw_ref[:, im_cols]
        cre = jnp.zeros((1, S5_HSTATE), F32)
        cim = jnp.zeros((1, S5_HSTATE), F32)
        for g in range(groups):
            hr = re3[g] + (pr * cre - pi * cim)
            hi = im3[g] + (pr * cim + pi * cre)
            hs_sc[0, 8 * g:8 * g + 8, :] = hr
            hs_sc[1, 8 * g:8 * g + 8, :] = hi
            cre, cim = hr[7:8, :], hi[7:8, :]
        hprev = jnp.concatenate([_shift_rows(hs_sc[0], 1), _shift_rows(hs_sc[1], 1)], axis=1)
        y = y + jnp.dot(hprev.astype(BF16), cm_ref[h], preferred_element_type=F32)
        y_halves.append(y)
    outs = []
    for t in range(S5_CHUNK):
        yt = jnp.concatenate([y_halves[0][:, t * 128:(t + 1) * 128], y_halves[1][:, t * 128:(t + 1) * 128]], axis=1)
        yt = _gelu(yt + d_ref[...] * u[:, t * D_C:(t + 1) * D_C])
        gl = jnp.dot(yt.astype(BF16), wglu_ref[...], preferred_element_type=F32)
        outs.append(_rms_norm(yt * jax.nn.sigmoid(gl), gn_ref[...]))
    o_ref[...] = jnp.concatenate(outs, axis=1)


def _s5(u2, m, bm, cm, lev, pw, d, wglu, gn, l, rows_per_seq):
    n, width = u2.shape
    blk = pl.BlockSpec((rows_per_seq, width), lambda b: (b, 0))
    return pl.pallas_call(
        _s5_kernel,
        grid=(n // rows_per_seq,),
        in_specs=[blk] + [_layer_spec(a, l) for a in (m, bm, cm, lev, pw, d, wglu, gn)],
        out_specs=blk,
        out_shape=jax.ShapeDtypeStruct(u2.shape, F32),
        scratch_shapes=[pltpu.VMEM((2, rows_per_seq, S5_HSTATE), F32)],
        compiler_params=_cparams("parallel"),
        name="s5",
    )(u2, m, bm, cm, lev, pw, d, wglu, gn)


def _s5_prepare(a_re, a_im, log_dt, b_re, b_im, c_re, c_im):
    hi = lax.Precision.HIGHEST
    T = S5_CHUNK
    nl = a_re.shape[0]
    dt = jnp.exp(log_dt)[:, :, None]
    lr, li = a_re * dt, a_im * dt

    def cpow(tau):
        mag = jnp.exp(lr[:, None] * tau[None, :, None, None])
        ang = li[:, None] * tau[None, :, None, None]
        return mag * jnp.cos(ang), mag * jnp.sin(ang)

    def halves(z):
        return jnp.moveaxis(z.reshape(z.shape[:2] + (2, S5_HALF) + z.shape[3:]), 2, 1)

    ar1, ai1 = cpow(jnp.ones((1,), F32))
    nr, ni = ar1[:, 0] - 1.0, ai1[:, 0]
    den = a_re * a_re + a_im * a_im
    fr, fi = (nr * a_re + ni * a_im) / den, (ni * a_re - nr * a_im) / den
    bbr = fr[..., None] * b_re - fi[..., None] * b_im
    bbi = fr[..., None] * b_im + fi[..., None] * b_re

    dr, di = cpow((T - 1) - jnp.arange(T, dtype=F32))
    abr = dr[..., None] * bbr[:, None] - di[..., None] * bbi[:, None]
    abi = dr[..., None] * bbi[:, None] + di[..., None] * bbr[:, None]

    def same_group(rows, rdiv, cols, cdiv, cmod=S5_HALF):
        r = (jnp.arange(rows) // rdiv)[:, None]
        c = ((jnp.arange(cols) // cdiv) % cmod)[None, :]
        return (r == c).astype(F32)

    def tile_eye(n, reps):
        return (jnp.arange(n)[:, None] == (jnp.arange(n * reps) % n)[None, :]).astype(F32)

    kern = (jnp.einsum('lgop,ljgpi->ljgio', c_re, abr, precision=hi)
            - jnp.einsum('lgop,ljgpi->ljgio', c_im, abi, precision=hi))
    kd = jnp.einsum('lhjrc,cn->lhjrn', halves(kern).reshape(nl, 2, T, 128, C_GROUP), tile_eye(C_GROUP, S5_HALF))
    kd = kd * same_group(128, C_GROUP, 128, C_GROUP)
    ext = jnp.concatenate([kd, jnp.zeros((nl, 2, T - 1, 128, 128), F32)], axis=2)
    m = jnp.concatenate([ext[:, :, T - 1 - b:2 * T - 1 - b] for b in range(T)], axis=-1)
    m = m.reshape(nl, 2, S5_HCOLS, S5_HCOLS)

    def state_in(ab):
        y = jnp.swapaxes(halves(ab), -1, -2).reshape(nl, 2, T, 128, C_STATE)
        return jnp.einsum('lhjrp,pn->lhjrn', y, tile_eye(C_STATE, S5_HALF)) * same_group(128, C_GROUP, S5_HSTATE, C_STATE)

    bm = jnp.concatenate([state_in(abr), state_in(abi)], axis=-1).reshape(nl, 2, S5_HCOLS, 2 * S5_HSTATE)

    ur, ui = cpow(jnp.arange(1, T + 1, dtype=F32))
    so_r = c_re[:, None] * ur[:, :, :, None, :] - c_im[:, None] * ui[:, :, :, None, :]
    so_i = c_re[:, None] * ui[:, :, :, None, :] + c_im[:, None] * ur[:, :, :, None, :]
    expand = (jnp.arange(T * C_GROUP)[:, None] // C_GROUP == jnp.arange(S5_HCOLS)[None, :] // 128) & (
        jnp.arange(T * C_GROUP)[:, None] % C_GROUP == jnp.arange(S5_HCOLS)[None, :] % C_GROUP)

    def state_out(so):
        y = jnp.transpose(halves(so), (0, 1, 3, 5, 2, 4)).reshape(nl, 2, S5_HSTATE, T * C_GROUP)
        return jnp.einsum('lhrc,cn->lhrn', y, expand.astype(F32)) * same_group(S5_HSTATE, C_STATE, S5_HCOLS, C_GROUP)

    cm = jnp.concatenate([state_out(so_r), state_out(-so_i)], axis=2)

    def state_layout(zr, zi):
        n = zr.shape[1]
        parts = []
        for h in range(2):
            gs = slice(h * S5_HALF, (h + 1) * S5_HALF)
            parts += [zr[:, :, gs].reshape(nl, n, S5_HSTATE), zi[:, :, gs].reshape(nl, n, S5_HSTATE)]
        return jnp.concatenate(parts, axis=-1)

    lev = state_layout(*cpow(jnp.asarray([T, 2 * T, 4 * T], F32)))
    pw = state_layout(*cpow(T * jnp.arange(1, 9, dtype=F32)))
    return m.astype(BF16), bm.astype(BF16), cm.astype(BF16), lev, pw


def _mix_ffn_kernel(oa_ref, ob_ref, oc_ref, x_ref, gb_ref, wo_ref, g2_ref, b2_ref, wg_ref, wu_ref, wd_ref,
                    g3_ref, b3_ref, o_ref, oc_sc):
    tm = x_ref.shape[0]
    for t in range(S5_CHUNK):
        for j in range(2):
            oc_sc[j, pl.ds(t, tm // S5_CHUNK, stride=S5_CHUNK), :] = oc_ref[:, t * D_C + j * 128:t * D_C + (j + 1) * 128]
    chains = _chain_rows(tm)
    mixes = []
    for rows in chains:
        nb = _rms_norm(ob_ref[rows, :], gb_ref[...]).astype(BF16)
        o = jnp.concatenate([oa_ref[rows, :], nb, oc_sc[0, rows, :].astype(BF16), oc_sc[1, rows, :].astype(BF16)], axis=1)
        mixes.append(jnp.dot(o, wo_ref[...], preferred_element_type=F32))
    xs = [_layer_norm(ALPHA * x_ref[rows, :] + mix, g2_ref[...], b2_ref[...]) for rows, mix in zip(chains, mixes)]
    for rows, out in zip(chains, _ffn_ln_chains(xs, wg_ref, wu_ref, wd_ref, g3_ref, b3_ref)):
        o_ref[rows, :] = out


def _mix_ffn(oa2, ob2, oc2, x2d, gb, wo, g2, b2, wg, wu, wd, g3, b3, l, tm):
    tok = lambda rows, width: pl.BlockSpec((rows, width), lambda i: (i, 0))
    return pl.pallas_call(
        _mix_ffn_kernel,
        grid=(x2d.shape[0] // tm,),
        in_specs=[tok(tm, D_A), tok(tm, D_B), tok(tm // S5_CHUNK, S5_CHUNK * D_C), tok(tm, D_MODEL)]
                 + [_layer_spec(a, l) for a in (gb, wo, g2, b2, wg, wu, wd, g3, b3)],
        out_specs=tok(tm, D_MODEL),
        out_shape=jax.ShapeDtypeStruct(x2d.shape, F32),
        scratch_shapes=[pltpu.VMEM((2, tm, 128), F32)],
        compiler_params=_cparams("parallel"),
        name="mix_ffn_ln",
    )(oa2, ob2, oc2, x2d, gb, wo, g2, b2, wg, wu, wd, g3, b3)


def _block_diag(w):
    eye = jnp.eye(A_BLOCKS, dtype=w.dtype)
    return jnp.einsum('lhij,hk->lhikj', w, eye).reshape(w.shape[0], D_A, D_A)


def _pick(n, pref):
    t = min(n, pref)
    while n % t:
        t //= 2
    return t


def kernel(x, ffn1_w_gate, ffn1_w_up, ffn1_w_down, ln1_g, ln1_b, w_in, conv_w, conv_b, rg_w_a, rg_b_a, rg_w_x, rg_b_x, rg_lambda, fox_b_f, s5_a_re, s5_a_im, s5_log_dt, s5_b_re, s5_b_im, s5_c_re, s5_c_im, s5_d, s5_w_glu, mix_norm_g, w_out, ln2_g, ln2_b, ffn2_w_gate, ffn2_w_up, ffn2_w_down, ln3_g, ln3_b):
    bsz, seq, _ = x.shape
    n = bsz * seq
    tm = _pick(seq, 512)
    vec = lambda v: v[:, None, :]
    bf = lambda w: w.astype(BF16)

    f1, f2 = [bf(w) for w in (ffn1_w_gate, ffn1_w_up, ffn1_w_down)], [bf(w) for w in (ffn2_w_gate, ffn2_w_up, ffn2_w_down)]
    cuts = np.cumsum([D_A, D_A, D_B, D_B, D_B, B_HEADS])
    wa = bf(w_in[:, :, :W_SPLIT])
    wt = bf(w_in[:, :, W_SPLIT:])
    place = np.zeros((B_HEADS, 128), np.float32)
    for piece in range(3):
        place[np.arange(B_HEADS), F_LANE0 + 8 * piece + np.arange(B_HEADS)] = 1.0
    place = jnp.asarray(place)
    hsel = np.zeros((2 * HEAD_PAIRS * 128, 128), np.float32)
    for p in range(HEAD_PAIRS):
        for is_k in range(2):
            for e in range(2):
                r0 = (2 * p + is_k) * 128 + e * B_HEAD_DIM
                hsel[r0:r0 + B_HEAD_DIM, F_LANE0 + 8 * is_k + 2 * p + e] = 1.0
    hsel = jnp.asarray(hsel, BF16)
    fcols = jnp.einsum('ldh,hn->ldn', wt[:, :, cuts[4] - W_SPLIT:cuts[5] - W_SPLIT], place.astype(BF16))
    wb = jnp.concatenate([wt[:, :, :cuts[4] - W_SPLIT], wt[:, :, cuts[5] - W_SPLIT:], fcols], axis=-1)
    bf_row = jnp.einsum('lh,hn->ln', fox_b_f, place, precision=lax.Precision.HIGHEST)[:, None, :]
    fmask = jnp.max(place, axis=0, keepdims=True)
    wgates = bf(jnp.concatenate([_block_diag(rg_w_a), _block_diag(rg_w_x)], axis=-1))
    bgates = jnp.concatenate([rg_b_a, rg_b_x], axis=-1)[:, None, :]
    g_a, g_b, g_c = (vec(g) for g in jnp.split(mix_norm_g, [D_A, D_A + D_B], axis=-1))
    sp = vec(jax.nn.softplus(-rg_lambda))
    m, bm, cm, lev, pw = _s5_prepare(s5_a_re, s5_a_im, s5_log_dt, s5_b_re, s5_b_im, s5_c_re, s5_c_im)
    wglu, wo = bf(s5_w_glu), bf(w_out)

    h = x.reshape(n, D_MODEL)
    for l in range(DEPTH):
        h = _ffn_ln(h, f1[0], f1[1], f1[2], vec(ln1_g), vec(ln1_b), l, _pick(n, 2 * tm))
        oa, qa, ka, v, cu, base, skip = _in_proj(h, wa, wb, bf_row, fmask, place, hsel, conv_w, vec(conv_b), wgates,
                                                 bgates, sp, g_a, l, bsz, seq, tm)
        skip = jnp.transpose(skip[:, :, 0, F_LANE0:F_LANE0 + B_HEADS:2], (0, 2, 1)).astype(jnp.int32)
        ob = _fox(skip, qa, ka, v.reshape(bsz, seq, D_B), base, tm)
        oc = _s5(cu, m, bm, cm, lev, pw, vec(s5_d), wglu, g_c, l, seq // S5_CHUNK)
        h = _mix_ffn(oa, ob.reshape(n, D_B), oc, h, g_b, wo, vec(ln2_g), vec(ln2_b), f2[0], f2[1], f2[2],
                     vec(ln3_g), vec(ln3_b), l, tm)
    return h.reshape(bsz, seq, D_MODEL)
```
